```python
import jax, jax.numpy as jnp
from jax import lax
import numpy as np

D_MODEL = 1024
BATCH = 8
SEQ = 4096
DEPTH = 4

POOL_GROUPS = 4
POOL_GROUP_DIM = D_MODEL // 16
POOL_WIDTH = POOL_GROUPS * POOL_GROUP_DIM
POOL_WINDOWS = (2, 4, 8, 16)

MLA_HEADS = 8
MLA_NOPE_DIM = D_MODEL // 16
MLA_ROPE_DIM = D_MODEL // 32
MLA_V_DIM = D_MODEL // 16
MLA_Q_RANK = 3 * D_MODEL // 8
MLA_KV_RANK = D_MODEL // 4
MLA_QK_DIM = MLA_NOPE_DIM + MLA_ROPE_DIM
MLA_WIDTH = MLA_HEADS * MLA_V_DIM
ROPE_THETA = 10000.0
Q_BLOCK = 128

LRU_WIDTH = D_MODEL // 4
LRU_HEADS = 4
LRU_HEAD_DIM = LRU_WIDTH // LRU_HEADS
CONV_WIDTH = 4
LRU_C = 8.0

N_BRANCH = 3
D_FF = 4 * D_MODEL
EPS = 1e-6

IN_POOL_END = POOL_WIDTH
IN_Q_END = IN_POOL_END + MLA_Q_RANK
IN_KV_END = IN_Q_END + MLA_KV_RANK + MLA_ROPE_DIM
IN_LRU_END = IN_KV_END + LRU_WIDTH
IN_COLS = IN_LRU_END + N_BRANCH * D_MODEL

kernel_name = "hybrid_pool_mla_rglru_gated"


def rms_norm(x, g):
    xf = x.astype(jnp.float32)
    y = xf * lax.rsqrt(jnp.mean(xf * xf, axis=-1, keepdims=True) + EPS)
    return (y * g.astype(jnp.float32)).astype(x.dtype)


def rope_tables(seq):
    pos = jnp.arange(seq, dtype=jnp.float32)
    inv = ROPE_THETA ** (-jnp.arange(0, MLA_ROPE_DIM, 2, dtype=jnp.float32) / MLA_ROPE_DIM)
    ang = pos[:, None] * inv[None, :]
    return jnp.cos(ang), jnp.sin(ang)


def apply_rope(x, cos, sin):
    x1, x2 = jnp.split(x.astype(jnp.float32), 2, axis=-1)
    shape = (cos.shape[0],) + (1,) * (x.ndim - 3) + (cos.shape[1],)
    c, s = cos.reshape(shape), sin.reshape(shape)
    return jnp.concatenate([x1 * c - x2 * s, x2 * c + x1 * s], axis=-1).astype(x.dtype)


def pool_mixer(u, w_grp, scale):
    B, S, _ = u.shape
    ug = u.reshape(B, S, POOL_GROUPS, POOL_GROUP_DIM)
    cs0 = jnp.pad(jnp.cumsum(ug.astype(jnp.float32), axis=1), ((0, 0), (1, 0), (0, 0), (0, 0)))
    t = jnp.arange(S, dtype=jnp.int32)[:, None]
    win = jnp.array(POOL_WINDOWS, dtype=jnp.int32)[None, :]
    start = jnp.maximum(t + 1 - win, 0)
    lagged = cs0[:, start, jnp.arange(POOL_GROUPS)[None, :], :]
    count = jnp.minimum(t + 1, win).astype(jnp.float32)[None, :, :, None]
    mixed = ((cs0[:, 1:] - lagged) / count - ug.astype(jnp.float32)).astype(u.dtype)
    y = jnp.einsum('bsgc,gcd->bsgd', mixed, w_grp)
    return y.reshape(B, S, POOL_WIDTH) * scale


def causal_attention(q, k, v):
    B, S, H, Dk = q.shape
    nb = S // Q_BLOCK
    scale = Dk ** -0.5
    qb = q.reshape(B, nb, Q_BLOCK, H, Dk).transpose(1, 0, 2, 3, 4)
    k_pos = jnp.arange(S, dtype=jnp.int32)

    def block(args):
        q_blk, i = args
        s = jnp.einsum('bqhd,bkhd->bhqk', q_blk, k, preferred_element_type=jnp.float32) * scale
        q_pos = i * Q_BLOCK + jnp.arange(Q_BLOCK, dtype=jnp.int32)
        s = jnp.where(k_pos[None, :] <= q_pos[:, None], s, -jnp.inf)
        p = jax.nn.softmax(s, axis=-1).astype(v.dtype)
        return jnp.einsum('bhqk,bkhd->bqhd', p, v)

    o = lax.map(block, (qb, jnp.arange(nb, dtype=jnp.int32)))
    return o.transpose(1, 0, 2, 3, 4).reshape(B, S, H, v.shape[-1])


def mla_mixer(q_lat, kv_lat, g_q, w_q_up, g_kv, w_kv_up, cos, sin):
    B, S, _ = q_lat.shape
    q = (rms_norm(q_lat, g_q) @ w_q_up).reshape(B, S, MLA_HEADS, MLA_QK_DIM)
    q_nope, q_rope = q[..., :MLA_NOPE_DIM], q[..., MLA_NOPE_DIM:]
    c_kv, k_rope = kv_lat[..., :MLA_KV_RANK], kv_lat[..., MLA_KV_RANK:]
    kv = (rms_norm(c_kv, g_kv) @ w_kv_up).reshape(B, S, MLA_HEADS, MLA_NOPE_DIM + MLA_V_DIM)
    k_nope, v = kv[..., :MLA_NOPE_DIM], kv[..., MLA_NOPE_DIM:]
    q_rope = apply_rope(q_rope, cos, sin)
    k_rope = apply_rope(k_rope, cos, sin)
    q = jnp.concatenate([q_nope, q_rope], axis=-1)
    k = jnp.concatenate([k_nope, jnp.broadcast_to(k_rope[:, :, None, :], (B, S, MLA_HEADS, MLA_ROPE_DIM))], axis=-1)
    o = causal_attention(q, k, v)
    return o.reshape(B, S, MLA_WIDTH)


def rg_lru_mixer(u, conv_w, conv_b, w_a, b_a, w_x, b_x, lam):
    B, S, C = u.shape
    u = lax.conv_general_dilated(u, conv_w[:, None, :], window_strides=(1,),
                                 padding=[(CONV_WIDTH - 1, 0)],
                                 dimension_numbers=('NWC', 'WIO', 'NWC'),
                                 feature_group_count=C) + conv_b
    ub = u.reshape(B, S, LRU_HEADS, LRU_HEAD_DIM)
    r = jax.nn.sigmoid(jnp.einsum('bshc,hcd->bshd', ub, w_a).reshape(B, S, C) + b_a)
    i = jax.nn.sigmoid(jnp.einsum('bshc,hcd->bshd', ub, w_x).reshape(B, S, C) + b_x)
    log_a = -LRU_C * r.astype(jnp.float32) * jax.nn.softplus(-lam.astype(jnp.float32))
    a = jnp.exp(log_a)
    b = jnp.sqrt(-jnp.expm1(2.0 * log_a)) * (i * u).astype(jnp.float32)

    def combine(left, right):
        a_l, b_l = left
        a_r, b_r = right
        return a_l * a_r, a_r * b_l + b_r

    _, h = lax.associative_scan(combine, (a, b), axis=1)
    return h.astype(u.dtype)


def setup_inputs(seed: int = 0) -> dict:
    key = jax.random.key(seed)
    ks = jax.random.split(key, 26)
    f32 = jnp.float32

    def nrm(k, shape, fan_in):
        return jax.random.normal(k, shape, f32) * (fan_in ** -0.5)

    def gain(k, shape):
        return 1.0 + 0.05 * jax.random.normal(k, shape, f32)

    def small(k, shape):
        return 0.01 * jax.random.normal(k, shape, f32)

    a8 = jax.random.uniform(ks[17], (DEPTH, LRU_WIDTH), f32, minval=0.9, maxval=0.999)
    s = a8 ** (1.0 / LRU_C)
    lru_lambda = jnp.log(s) - jnp.log1p(-s)

    return {
        "x": jax.random.normal(ks[0], (BATCH, SEQ, D_MODEL), f32),
        "g_mix": gain(ks[1], (DEPTH, D_MODEL)),
        "w_in": nrm(ks[2], (DEPTH, D_MODEL, IN_COLS), D_MODEL),
        "w_pool_grp": nrm(ks[3], (DEPTH, POOL_GROUPS, POOL_GROUP_DIM, POOL_GROUP_DIM), POOL_GROUP_DIM),
        "pool_scale": 1.0 + 0.1 * jax.random.normal(ks[4], (DEPTH, POOL_WIDTH), f32),
        "w_pool_proj": nrm(ks[5], (DEPTH, POOL_WIDTH, D_MODEL), POOL_WIDTH),
        "g_q": gain(ks[6], (DEPTH, MLA_Q_RANK)),
        "w_q_up": nrm(ks[7], (DEPTH, MLA_Q_RANK, MLA_HEADS * MLA_QK_DIM), MLA_Q_RANK),
        "g_kv": gain(ks[8], (DEPTH, MLA_KV_RANK)),
        "w_kv_up": nrm(ks[9], (DEPTH, MLA_KV_RANK, MLA_HEADS * (MLA_NOPE_DIM + MLA_V_DIM)), MLA_KV_RANK),
        "w_mla_o": nrm(ks[10], (DEPTH, MLA_WIDTH, D_MODEL), MLA_WIDTH),
        "conv_w": nrm(ks[11], (DEPTH, CONV_WIDTH, LRU_WIDTH), CONV_WIDTH),
        "conv_b": small(ks[12], (DEPTH, LRU_WIDTH)),
        "w_lru_a": nrm(ks[13], (DEPTH, LRU_HEADS, LRU_HEAD_DIM, LRU_HEAD_DIM), LRU_HEAD_DIM),
        "b_lru_a": small(ks[14], (DEPTH, LRU_WIDTH)),
        "w_lru_x": nrm(ks[15], (DEPTH, LRU_HEADS, LRU_HEAD_DIM, LRU_HEAD_DIM), LRU_HEAD_DIM),
        "b_lru_x": small(ks[16], (DEPTH, LRU_WIDTH)),
        "lru_lambda": lru_lambda,
        "w_lru_proj": nrm(ks[18], (DEPTH, LRU_WIDTH, D_MODEL), LRU_WIDTH),
        "b_gate": small(ks[19], (DEPTH, N_BRANCH * D_MODEL)),
        "w_out": nrm(ks[20], (DEPTH, D_MODEL, D_MODEL), D_MODEL),
        "g_ffn": gain(ks[21], (DEPTH, D_MODEL)),
        "w_ff1": nrm(ks[22], (DEPTH, D_MODEL, D_FF), D_MODEL),
        "w_ff2": nrm(ks[23], (DEPTH, D_FF, D_MODEL), D_FF),
        "g_final": gain(ks[24], (D_MODEL,)),
    }


def reference(x, g_mix, w_in, w_pool_grp, pool_scale, w_pool_proj, g_q, w_q_up, g_kv, w_kv_up,
              w_mla_o, conv_w, conv_b, w_lru_a, b_lru_a, w_lru_x, b_lru_x, lru_lambda, w_lru_proj,
              b_gate, w_out, g_ffn, w_ff1, w_ff2, g_final):
    B, S, D = x.shape
    cos, sin = rope_tables(S)
    for l in range(DEPTH):
        h = rms_norm(x, g_mix[l])
        proj = h @ w_in[l]
        u_pool, q_lat, kv_lat, u_lru, gate_logit = jnp.split(
            proj, [IN_POOL_END, IN_Q_END, IN_KV_END, IN_LRU_END], axis=-1)
        y_pool = pool_mixer(u_pool, w_pool_grp[l], pool_scale[l]) @ w_pool_proj[l]
        y_mla = mla_mixer(q_lat, kv_lat, g_q[l], w_q_up[l], g_kv[l], w_kv_up[l], cos, sin) @ w_mla_o[l]
        y_lru = rg_lru_mixer(u_lru, conv_w[l], conv_b[l], w_lru_a[l], b_lru_a[l],
                             w_lru_x[l], b_lru_x[l], lru_lambda[l]) @ w_lru_proj[l]
        gates = jax.nn.sigmoid(gate_logit + b_gate[l]).reshape(B, S, N_BRANCH, D)
        merged = gates[:, :, 0] * y_pool + gates[:, :, 1] * y_mla + gates[:, :, 2] * y_lru
        x = x + merged @ w_out[l]
        h = rms_norm(x, g_ffn[l])
        x = x + jnp.square(jax.nn.relu(h @ w_ff1[l])) @ w_ff2[l]
    return rms_norm(x, g_final)
```

```python
import functools

import jax
import jax.numpy as jnp
from jax import lax
from jax.experimental import pallas as pl
from jax.experimental.pallas import tpu as pltpu

F32 = jnp.float32
BF16 = jnp.bfloat16

D_MODEL = 1024
DEPTH = 4
POOL_GROUPS = 4
POOL_GROUP_DIM = 64
POOL_WIDTH = 256
POOL_WINDOWS = (2, 4, 8, 16)
MLA_HEADS = 8
MLA_NOPE_DIM = 64
MLA_ROPE_DIM = 32
MLA_V_DIM = 64
MLA_Q_RANK = 384
MLA_KV_RANK = 256
MLA_QK_DIM = MLA_NOPE_DIM + MLA_ROPE_DIM
MLA_WIDTH = MLA_HEADS * MLA_V_DIM
ROPE_THETA = 10000.0
LRU_WIDTH = 256
CONV_WIDTH = 4
LRU_C = 8.0
D_FF = 4 * D_MODEL
EPS = 1e-6

IN_POOL_END = POOL_WIDTH
IN_Q_END = IN_POOL_END + MLA_Q_RANK
IN_KV_END = IN_Q_END + MLA_KV_RANK + MLA_ROPE_DIM
IN_LRU_END = IN_KV_END + LRU_WIDTH

LANES = 128
HEAD_PAD = LANES
HALO = 16
SMALL_COLS = POOL_WIDTH + LRU_WIDTH + MLA_Q_RANK + MLA_KV_RANK + LANES
VMEM_LIMIT_BYTES = 56 * 1024 * 1024
MASK_VALUE = -1e30

SEQ_TILE = 512
ATTN_TILE = 512
FFN_TILE = 512
FF_CHUNK = 1024


def _rms(x, g):
    return x * lax.rsqrt(jnp.mean(x * x, axis=-1, keepdims=True) + EPS) * g


def _sigmoid(x):
    return 1.0 / (1.0 + jnp.exp(-x))


def _rope(blk, cos, sa, sb):
    return blk * cos + pltpu.roll(blk, 16, 1) * sa + pltpu.roll(blk, LANES - 16, 1) * sb


def _mix_kernel(x_ref, gmix_ref, wsm_ref, wg_ref, bg_ref, cos_ref, sa_ref, sb_ref,
                wpg_ref, pscale_ref, wpp_ref, gq_ref, wq_ref, gkv_ref, wk_ref, wv_ref,
                convw_ref, convb_ref, wax_ref, bax_ref, lam_ref, wlp_ref,
                q_ref, k_ref, v_ref, part_ref, g1_ref,
                halo_ref, h_ref, *, ts):
    s = pl.program_id(1)

    @pl.when(s == 0)
    def _():
        halo_ref[...] = jnp.zeros_like(halo_ref)
        h_ref[...] = jnp.zeros_like(h_ref)

    hb = _rms(x_ref[...], gmix_ref[...]).astype(BF16)
    small = jnp.dot(hb, wsm_ref[...], preferred_element_type=F32)

    u2 = small[:, :POOL_WIDTH + LRU_WIDTH]
    ext = jnp.concatenate([halo_ref[...], u2], axis=0)
    halo_ref[...] = u2[ts - HALO:, :]
    ep = ext[:, :POOL_WIDTH]
    el = ext[:, POOL_WIDTH:]

    s2 = ep + pltpu.roll(ep, 1, 0)
    s4 = s2 + pltpu.roll(s2, 2, 0)
    s8 = s4 + pltpu.roll(s4, 4, 0)
    s16 = s8 + pltpu.roll(s8, 8, 0)
    lane = lax.broadcasted_iota(jnp.int32, (1, POOL_WIDTH), 1)
    grp = lane // POOL_GROUP_DIM
    wsum = jnp.where(grp == 0, s2, jnp.where(grp == 1, s4, jnp.where(grp == 2, s8, s16)))[HALO:]
    win = jnp.where(grp == 0, float(POOL_WINDOWS[0]),
                    jnp.where(grp == 1, float(POOL_WINDOWS[1]),
                              jnp.where(grp == 2, float(POOL_WINDOWS[2]), float(POOL_WINDOWS[3]))))
    row = lax.broadcasted_iota(jnp.int32, (ts, 1), 0)
    tpos = (s * ts + row + 1).astype(F32)
    cnt = jnp.minimum(tpos, win)
    mixed = wsum / cnt - u2[:, :POOL_WIDTH]
    yp = jnp.dot(mixed.astype(BF16), wpg_ref[...], preferred_element_type=F32) * pscale_ref[...]
    y_pool = jnp.dot(yp.astype(BF16), wpp_ref[...], preferred_element_type=F32)

    cw = convw_ref[...]
    uc = (cw[3:4, :] * el + cw[2:3, :] * pltpu.roll(el, 1, 0)
          + cw[1:2, :] * pltpu.roll(el, 2, 0) + cw[0:1, :] * pltpu.roll(el, 3, 0))
    uc = uc[HALO:] + convb_ref[...]
    ri = _sigmoid(jnp.dot(uc.astype(BF16), wax_ref[...], preferred_element_type=F32) + bax_ref[...])
    r = ri[:, :LRU_WIDTH]
    ig = ri[:, LRU_WIDTH:]
    nlam = -lam_ref[...]
    softplus = jnp.maximum(nlam, 0.0) + jnp.log1p(jnp.exp(-jnp.abs(nlam)))
    log_a = (-LRU_C) * r * softplus
    a_cum = jnp.exp(log_a)
    b_cum = jnp.sqrt(-jnp.tanh(log_a) * (a_cum * a_cum + 1.0)) * (ig * uc)
    k = 1
    while k < ts:
        keep = row >= k
        a_sh = jnp.where(keep, pltpu.roll(a_cum, k, 0), 1.0)
        b_sh = jnp.where(keep, pltpu.roll(b_cum, k, 0), 0.0)
        b_cum = a_cum * b_sh + b_cum
        a_cum = a_cum * a_sh
        k *= 2
    hh = a_cum * h_ref[0:1, :] + b_cum
    h_ref[...] = jnp.broadcast_to(hh[ts - 1:ts, :], h_ref.shape)
    y_lru = jnp.dot(hh.astype(BF16), wlp_ref[...], preferred_element_type=F32)

    def gate(j):
        z = jnp.dot(hb, wg_ref[:, j * D_MODEL:(j + 1) * D_MODEL], preferred_element_type=F32)
        return _sigmoid(z + bg_ref[:, j * D_MODEL:(j + 1) * D_MODEL])

    part_ref[...] = gate(0) * y_pool + gate(2) * y_lru
    g1_ref[...] = gate(1).astype(BF16)

    cos = cos_ref[...]
    sa = sa_ref[...]
    sb = sb_ref[...]
    q0 = POOL_WIDTH + LRU_WIDTH
    c0 = q0 + MLA_Q_RANK
    r0 = c0 + MLA_KV_RANK
    qn = _rms(small[:, q0:c0], gq_ref[...]) * (MLA_QK_DIM ** -0.5)
    qf = jnp.dot(qn.astype(BF16), wq_ref[...], preferred_element_type=F32)
    cn = _rms(small[:, c0:r0], gkv_ref[...]).astype(BF16)
    kf = jnp.dot(cn, wk_ref[...], preferred_element_type=F32)
    kr = _rope(small[:, r0:r0 + LANES], cos, sa, sb)
    for hd in range(MLA_HEADS):
        sl = slice(hd * HEAD_PAD, (hd + 1) * HEAD_PAD)
        q_ref[:, sl] = _rope(qf[:, sl], cos, sa, sb).astype(BF16)
        k_ref[:, sl] = (kf[:, sl] + kr).astype(BF16)
    v_ref[...] = jnp.dot(cn, wv_ref[...], preferred_element_type=F32).astype(BF16)


def _attn_kernel(q_ref, k_ref, v_ref, o_ref, m_ref, l_ref, acc_ref, *, tq):
    i = pl.program_id(2)
    row = lax.broadcasted_iota(jnp.int32, (tq, tq), 0)
    col = lax.broadcasted_iota(jnp.int32, (tq, tq), 1)
    causal = col <= row
    outs = []
    for j in range(2):
        hs = slice(j * HEAD_PAD, (j + 1) * HEAD_PAD)
        q = q_ref[:, hs]
        m_ref[j] = jnp.full((tq, 1), MASK_VALUE, F32)
        l_ref[j] = jnp.zeros((tq, 1), F32)
        acc_ref[j] = jnp.zeros((tq, LANES), F32)

        def update(c, masked, j=j, hs=hs, q=q):
            start = pl.multiple_of(c * tq, tq)
            kc = k_ref[pl.ds(start, tq), hs]
            vc = v_ref[pl.ds(start, tq), :]
            sc = lax.dot_general(q, kc, (((1,), (1,)), ((), ())), preferred_element_type=F32)
            if masked:
                sc = jnp.where(causal, sc, MASK_VALUE)
            m_old = m_ref[j]
            m_new = jnp.maximum(m_old, jnp.max(sc, axis=-1, keepdims=True))
            p = jnp.exp(sc - m_new)
            alpha = jnp.exp(m_old - m_new)
            l_ref[j] = alpha * l_ref[j] + jnp.sum(p, axis=-1, keepdims=True)
            acc_ref[j] = alpha * acc_ref[j] + jnp.dot(p.astype(BF16), vc, preferred_element_type=F32)
            m_ref[j] = m_new

        def body(c, carry, update=update):
            update(c, False)
            return carry

        lax.fori_loop(0, i, body, 0)
        update(i, True)
        outs.append(acc_ref[j] / l_ref[j])
    lane = lax.broadcasted_iota(jnp.int32, (1, LANES), 1)
    o_ref[...] = jnp.where(lane < MLA_V_DIM, outs[0], outs[1]).astype(BF16)


def _ffn_kernel(x_ref, o_ref, part_ref, g1_ref, wo_ref, wout_ref, gffn_ref, w1_ref, w2_ref, gfin_ref,
                out_ref, *, final):
    y_mla = jnp.dot(o_ref[...], wo_ref[...], preferred_element_type=F32)
    merged = part_ref[...] + g1_ref[...].astype(F32) * y_mla
    x1 = x_ref[...] + jnp.dot(merged.astype(BF16), wout_ref[...], preferred_element_type=F32)
    hb = _rms(x1, gffn_ref[...]).astype(BF16)
    acc = x1
    for c in range(D_FF // FF_CHUNK):
        cs = slice(c * FF_CHUNK, (c + 1) * FF_CHUNK)
        f = jnp.dot(hb, w1_ref[:, cs], preferred_element_type=F32)
        f = jnp.square(jnp.maximum(f, 0.0)).astype(BF16)
        acc = acc + jnp.dot(f, w2_ref[cs, :], preferred_element_type=F32)
    if final:
        acc = _rms(acc, gfin_ref[...])
    out_ref[...] = acc


def _const_spec(shape):
    return pl.BlockSpec(shape, lambda *_: (0,) * len(shape), pipeline_mode=pl.Buffered(1))


def _params(n_axes):
    return pltpu.CompilerParams(dimension_semantics=("arbitrary",) * n_axes,
                                vmem_limit_bytes=VMEM_LIMIT_BYTES)


def _mix_call(x2, lw, tabs, batch, seq):
    ts = min(SEQ_TILE, seq)
    ns = seq // ts
    tok = batch * seq
    tile = lambda w: pl.BlockSpec((ts, w), lambda b, s: (b * ns + s, 0))
    tab = pl.BlockSpec((ts, LANES), lambda b, s: (s, 0))
    consts = [lw[n] for n in ("g_mix", "wsm", "wg", "bg")]
    consts2 = [lw[n] for n in ("wpg", "pscale", "wpp", "g_q", "wq", "g_kv", "wk", "wv",
                               "convw", "convb", "wax", "bax", "lam", "wlp")]
    in_specs = ([tile(D_MODEL)] + [_const_spec(c.shape) for c in consts] + [tab, tab, tab]
                + [_const_spec(c.shape) for c in consts2])
    out_shape = (jax.ShapeDtypeStruct((tok, MLA_HEADS * HEAD_PAD), BF16),
                 jax.ShapeDtypeStruct((tok, MLA_HEADS * HEAD_PAD), BF16),
                 jax.ShapeDtypeStruct((tok, MLA_WIDTH), BF16),
                 jax.ShapeDtypeStruct((tok, D_MODEL), F32),
                 jax.ShapeDtypeStruct((tok, D_MODEL), BF16))
    out_specs = (tile(MLA_HEADS * HEAD_PAD), tile(MLA_HEADS * HEAD_PAD), tile(MLA_WIDTH),
                 tile(D_MODEL), tile(D_MODEL))
    return pl.pallas_call(
        functools.partial(_mix_kernel, ts=ts),
        grid=(batch, ns),
        in_specs=in_specs,
        out_specs=out_specs,
        out_shape=out_shape,
        scratch_shapes=[pltpu.VMEM((HALO, POOL_WIDTH + LRU_WIDTH), F32),
                        pltpu.VMEM((8, LRU_WIDTH), F32)],
        compiler_params=_params(2),
        name="mix",
    )(x2, *consts, *tabs, *consts2)


def _attn_call(q, k, v, batch, seq):
    tq = min(ATTN_TILE, seq)
    nq = seq // tq
    tok = batch * seq
    pairs = MLA_HEADS // 2
    return pl.pallas_call(
        functools.partial(_attn_kernel, tq=tq),
        grid=(batch, pairs, nq),
        in_specs=[pl.BlockSpec((tq, 2 * HEAD_PAD), lambda b, p, i: (b * nq + i, p)),
                  pl.BlockSpec((seq, 2 * HEAD_PAD), lambda b, p, i: (b, p)),
                  pl.BlockSpec((seq, LANES), lambda b, p, i: (b, p))],
        out_specs=pl.BlockSpec((tq, LANES), lambda b, p, i: (b * nq + i, p)),
        out_shape=jax.ShapeDtypeStruct((tok, MLA_WIDTH), BF16),
        scratch_shapes=[pltpu.VMEM((2, tq, 1), F32), pltpu.VMEM((2, tq, 1), F32),
                        pltpu.VMEM((2, tq, LANES), F32)],
        compiler_params=_params(3),
        name="attn",
    )(q, k, v)


def _ffn_call(x2, o, part, g1, lw, g_final, final):
    tok = x2.shape[0]
    tm = min(FFN_TILE, tok)
    tile = lambda w: pl.BlockSpec((tm, w), lambda t: (t, 0))
    consts = [lw[n] for n in ("wo", "wout", "g_ffn", "w1", "w2")] + [g_final]
    return pl.pallas_call(
        functools.partial(_ffn_kernel, final=final),
        grid=(tok // tm,),
        in_specs=[tile(D_MODEL), tile(MLA_WIDTH), tile(D_MODEL), tile(D_MODEL)]
                 + [_const_spec(c.shape) for c in consts],
        out_specs=tile(D_MODEL),
        out_shape=jax.ShapeDtypeStruct((tok, D_MODEL), F32),
        compiler_params=_params(1),
        name="ffn",
    )(x2, o, part, g1, *consts)


def _block_diag(w):
    depth, g, c, _ = w.shape
    eye = jnp.eye(g, dtype=w.dtype)
    return jnp.einsum("lgcd,gh->lgchd", w, eye).reshape(depth, g * c, g * c)


def _rope_tables(seq):
    pos = jnp.arange(seq, dtype=F32)
    inv = ROPE_THETA ** (-jnp.arange(0, MLA_ROPE_DIM, 2, dtype=F32) / MLA_ROPE_DIM)
    ang = pos[:, None] * inv[None, :]
    cos, sin = jnp.cos(ang), jnp.sin(ang)
    half = MLA_ROPE_DIM // 2
    ones = jnp.ones((seq, MLA_NOPE_DIM), F32)
    zh = jnp.zeros((seq, half), F32)
    tail = jnp.zeros((seq, LANES - MLA_QK_DIM), F32)
    zn = jnp.zeros((seq, MLA_NOPE_DIM), F32)
    cos_t = jnp.concatenate([ones, cos, cos, tail], axis=1)
    sa_t = jnp.concatenate([zn, zh, sin, tail], axis=1)
    sb_t = jnp.concatenate([zn, -sin, zh, tail], axis=1)
    return cos_t, sa_t, sb_t


def _prep_weights(w_in, w_pool_grp, pool_scale, w_pool_proj, g_mix, g_q, w_q_up, g_kv, w_kv_up, w_mla_o,
                  conv_w, conv_b, w_lru_a, b_lru_a, w_lru_x, b_lru_x, lru_lambda, w_lru_proj, b_gate,
                  w_out, g_ffn, w_ff1, w_ff2):
    depth = w_in.shape[0]
    row = lambda a: a[:, None, :]
    w_pool = w_in[:, :, :IN_POOL_END]
    w_qlat = w_in[:, :, IN_POOL_END:IN_Q_END]
    w_ckv = w_in[:, :, IN_Q_END:IN_Q_END + MLA_KV_RANK]
    w_kr = w_in[:, :, IN_Q_END + MLA_KV_RANK:IN_KV_END]
    w_lru = w_in[:, :, IN_KV_END:IN_LRU_END]
    w_gate = w_in[:, :, IN_LRU_END:]
    kr_blk = jnp.pad(w_kr, ((0, 0), (0, 0), (MLA_NOPE_DIM, LANES - MLA_QK_DIM)))
    wsm = jnp.concatenate([w_pool, w_lru, w_qlat, w_ckv, kr_blk], axis=2).astype(BF16)
    wq = jnp.pad(w_q_up.reshape(depth, MLA_Q_RANK, MLA_HEADS, MLA_QK_DIM),
                 ((0, 0), (0, 0), (0, 0), (0, HEAD_PAD - MLA_QK_DIM)))
    wq = wq.reshape(depth, MLA_Q_RANK, MLA_HEADS * HEAD_PAD).astype(BF16)
    wkv = w_kv_up.reshape(depth, MLA_KV_RANK, MLA_HEADS, MLA_NOPE_DIM + MLA_V_DIM)
    wk = jnp.pad(wkv[..., :MLA_NOPE_DIM], ((0, 0), (0, 0), (0, 0), (0, HEAD_PAD - MLA_NOPE_DIM)))
    wk = wk.reshape(depth, MLA_KV_RANK, MLA_HEADS * HEAD_PAD).astype(BF16)
    wv = wkv[..., MLA_NOPE_DIM:].reshape(depth, MLA_KV_RANK, MLA_WIDTH).astype(BF16)
    wax = jnp.concatenate([_block_diag(w_lru_a), _block_diag(w_lru_x)], axis=2).astype(BF16)
    return dict(
        g_mix=row(g_mix), wsm=wsm, wg=w_gate.astype(BF16), bg=row(b_gate),
        wpg=_block_diag(w_pool_grp).astype(BF16), pscale=row(pool_scale), wpp=w_pool_proj.astype(BF16),
        g_q=row(g_q), wq=wq, g_kv=row(g_kv), wk=wk, wv=wv,
        convw=conv_w, convb=row(conv_b), wax=wax,
        bax=row(jnp.concatenate([b_lru_a, b_lru_x], axis=1)), lam=row(lru_lambda),
        wlp=w_lru_proj.astype(BF16),
        wo=w_mla_o.astype(BF16), wout=w_out.astype(BF16), g_ffn=row(g_ffn),
        w1=w_ff1.astype(BF16), w2=w_ff2.astype(BF16))


def kernel(x, g_mix, w_in, w_pool_grp, pool_scale, w_pool_proj, g_q, w_q_up, g_kv, w_kv_up, w_mla_o, conv_w, conv_b, w_lru_a, b_lru_a, w_lru_x, b_lru_x, lru_lambda, w_lru_proj, b_gate, w_out, g_ffn, w_ff1, w_ff2, g_final):
    batch, seq, d = x.shape
    assert d == D_MODEL and seq % min(SEQ_TILE, seq) == 0 and seq >= HALO
    depth = w_in.shape[0]
    weights = _prep_weights(w_in, w_pool_grp, pool_scale, w_pool_proj, g_mix, g_q, w_q_up, g_kv, w_kv_up,
                            w_mla_o, conv_w, conv_b, w_lru_a, b_lru_a, w_lru_x, b_lru_x, lru_lambda,
                            w_lru_proj, b_gate, w_out, g_ffn, w_ff1, w_ff2)
    tabs = _rope_tables(seq)
    gfin = g_final[None, :]
    x2 = x.reshape(batch * seq, d)
    for l in range(depth):
        lw = {n: w[l] for n, w in weights.items()}
        q, k, v, part, g1 = _mix_call(x2, lw, tabs, batch, seq)
        o = _attn_call(q, k, v, batch, seq)
        x2 = _ffn_call(x2, o, part, g1, lw, gfin, final=(l == depth - 1))
    return x2.reshape(batch, seq, d)
```

```python
import functools

import jax
import jax.numpy as jnp
from jax import lax
from jax.experimental import pallas as pl
from jax.experimental.pallas import tpu as pltpu

F32 = jnp.float32
BF16 = jnp.bfloat16

D_MODEL = 1024
DEPTH = 4
POOL_GROUPS = 4
POOL_GROUP_DIM = 64
POOL_WIDTH = 256
POOL_WINDOWS = (2, 4, 8, 16)
MLA_HEADS = 8
MLA_NOPE_DIM = 64
MLA_ROPE_DIM = 32
MLA_V_DIM = 64
MLA_Q_RANK = 384
MLA_KV_RANK = 256
MLA_QK_DIM = MLA_NOPE_DIM + MLA_ROPE_DIM
MLA_WIDTH = MLA_HEADS * MLA_V_DIM
ROPE_THETA = 10000.0
LRU_WIDTH = 256
CONV_WIDTH = 4
LRU_C = 8.0
D_FF = 4 * D_MODEL
EPS = 1e-6
LOG2_E = 1.4426950408889634

IN_POOL_END = POOL_WIDTH
IN_Q_END = IN_POOL_END + MLA_Q_RANK
IN_KV_END = IN_Q_END + MLA_KV_RANK + MLA_ROPE_DIM
IN_LRU_END = IN_KV_END + LRU_WIDTH

LANES = 128
HEAD_PAD = LANES
HALO = 16
SMALL_COLS = POOL_WIDTH + LRU_WIDTH + MLA_Q_RANK + MLA_KV_RANK + LANES
VMEM_LIMIT_BYTES = 56 * 1024 * 1024
MASK_VALUE = -1e30

SEQ_TILE = 512
ATTN_TILE = 512
ATTN_HEADS_PER_STEP = 4
FFN_TILE = 512
FF_CHUNK = 1024


def _rms(x, g):
    return x * lax.rsqrt(jnp.mean(x * x, axis=-1, keepdims=True) + EPS) * g


def _sigmoid(x):
    return 1.0 / (1.0 + jnp.exp(-x))


def _rope(blk, cos, sa, sb):
    return blk * cos + pltpu.roll(blk, 16, 1) * sa + pltpu.roll(blk, LANES - 16, 1) * sb


def _mix_kernel(x_ref, gmix_ref, wsm_ref, wg_ref, bg_ref, cos_ref, sa_ref, sb_ref,
                wpg_ref, pscale_ref, wpp_ref, gq_ref, wq_ref, gkv_ref, wk_ref, wv_ref,
                convw_ref, convb_ref, wax_ref, bax_ref, lam_ref, wlp_ref,
                q_ref, k_ref, v_ref, part_ref, g1_ref,
                halo_ref, h_ref, *, ts):
    s = pl.program_id(1)

    @pl.when(s == 0)
    def _():
        halo_ref[...] = jnp.zeros_like(halo_ref)
        h_ref[...] = jnp.zeros_like(h_ref)

    hb = _rms(x_ref[...], gmix_ref[...]).astype(BF16)
    small = jnp.dot(hb, wsm_ref[...], preferred_element_type=F32)

    u2 = small[:, :POOL_WIDTH + LRU_WIDTH]
    ext = jnp.concatenate([halo_ref[...], u2], axis=0)
    halo_ref[...] = u2[ts - HALO:, :]
    ep = ext[:, :POOL_WIDTH]
    el = ext[:, POOL_WIDTH:]

    s2 = ep + pltpu.roll(ep, 1, 0)
    s4 = s2 + pltpu.roll(s2, 2, 0)
    s8 = s4 + pltpu.roll(s4, 4, 0)
    s16 = s8 + pltpu.roll(s8, 8, 0)
    lane = lax.broadcasted_iota(jnp.int32, (1, POOL_WIDTH), 1)
    grp = lane // POOL_GROUP_DIM
    wsum = jnp.where(grp == 0, s2, jnp.where(grp == 1, s4, jnp.where(grp == 2, s8, s16)))[HALO:]
    win = jnp.where(grp == 0, float(POOL_WINDOWS[0]),
                    jnp.where(grp == 1, float(POOL_WINDOWS[1]),
                              jnp.where(grp == 2, float(POOL_WINDOWS[2]), float(POOL_WINDOWS[3]))))
    row = lax.broadcasted_iota(jnp.int32, (ts, 1), 0)
    tpos = (s * ts + row + 1).astype(F32)
    cnt = jnp.minimum(tpos, win)
    mixed = wsum / cnt - u2[:, :POOL_WIDTH]
    yp = jnp.dot(mixed.astype(BF16), wpg_ref[...], preferred_element_type=F32) * pscale_ref[...]
    y_pool = jnp.dot(yp.astype(BF16), wpp_ref[...], preferred_element_type=F32)

    cw = convw_ref[...]
    uc = (cw[3:4, :] * el + cw[2:3, :] * pltpu.roll(el, 1, 0)
          + cw[1:2, :] * pltpu.roll(el, 2, 0) + cw[0:1, :] * pltpu.roll(el, 3, 0))
    uc = uc[HALO:] + convb_ref[...]
    ri = _sigmoid(jnp.dot(uc.astype(BF16), wax_ref[...], preferred_element_type=F32) + bax_ref[...])
    r = ri[:, :LRU_WIDTH]
    ig = ri[:, LRU_WIDTH:]
    nlam = -lam_ref[...]
    softplus = jnp.maximum(nlam, 0.0) + jnp.log1p(jnp.exp(-jnp.abs(nlam)))
    log_a = (-LRU_C) * r * softplus
    a_cum = jnp.exp(log_a)
    b_cum = jnp.sqrt(-jnp.tanh(log_a) * (a_cum * a_cum + 1.0)) * (ig * uc)
    k = 1
    while k < ts:
        keep = row >= k
        a_sh = jnp.where(keep, pltpu.roll(a_cum, k, 0), 1.0)
        b_sh = jnp.where(keep, pltpu.roll(b_cum, k, 0), 0.0)
        b_cum = a_cum * b_sh + b_cum
        a_cum = a_cum * a_sh
        k *= 2
    hh = a_cum * h_ref[0:1, :] + b_cum
    h_ref[...] = jnp.broadcast_to(hh[ts - 1:ts, :], h_ref.shape)
    y_lru = jnp.dot(hh.astype(BF16), wlp_ref[...], preferred_element_type=F32)

    def gate(j):
        z = jnp.dot(hb, wg_ref[:, j * D_MODEL:(j + 1) * D_MODEL], preferred_element_type=F32)
        return _sigmoid(z + bg_ref[:, j * D_MODEL:(j + 1) * D_MODEL])

    part_ref[...] = gate(0) * y_pool + gate(2) * y_lru
    g1_ref[...] = gate(1).astype(BF16)

    cos = cos_ref[...]
    sa = sa_ref[...]
    sb = sb_ref[...]
    q0 = POOL_WIDTH + LRU_WIDTH
    c0 = q0 + MLA_Q_RANK
    r0 = c0 + MLA_KV_RANK
    qn = _rms(small[:, q0:c0], gq_ref[...]) * (MLA_QK_DIM ** -0.5 * LOG2_E)
    qf = jnp.dot(qn.astype(BF16), wq_ref[...], preferred_element_type=F32)
    cn = _rms(small[:, c0:r0], gkv_ref[...]).astype(BF16)
    kf = jnp.dot(cn, wk_ref[...], preferred_element_type=F32)
    kr = _rope(small[:, r0:r0 + LANES], cos, sa, sb)
    for hd in range(MLA_HEADS):
        sl = slice(hd * HEAD_PAD, (hd + 1) * HEAD_PAD)
        q_ref[:, sl] = _rope(qf[:, sl], cos, sa, sb).astype(BF16)
        k_ref[:, sl] = (kf[:, sl] + kr).astype(BF16)
    vlane = lax.broadcasted_iota(jnp.int32, (1, MLA_HEADS * HEAD_PAD), 1)
    vpos = vlane % (2 * HEAD_PAD)
    vone = jnp.where((vpos == MLA_V_DIM) | (vpos == HEAD_PAD), 1.0, 0.0)
    v_ref[...] = (jnp.dot(cn, wv_ref[...], preferred_element_type=F32) + vone).astype(BF16)


def _attn_kernel(q_ref, k_ref, v_ref, o_ref, m_ref, acc_ref, *, tq, heads):
    i = pl.program_id(2)
    row = lax.broadcasted_iota(jnp.int32, (tq, tq), 0)
    col = lax.broadcasted_iota(jnp.int32, (tq, tq), 1)
    causal = col <= row
    for j in range(heads):
        m_ref[j] = jnp.full((tq, LANES), MASK_VALUE, F32)
        acc_ref[j] = jnp.zeros((tq, LANES), F32)

    def update(c, masked):
        start = pl.multiple_of(c * tq, tq)
        for j in range(heads):
            hs = slice(j * HEAD_PAD, (j + 1) * HEAD_PAD)
            kc = k_ref[pl.ds(start, tq), hs]
            vc = v_ref[pl.ds(start, tq), hs]
            sc = lax.dot_general(q_ref[:, hs], kc, (((1,), (1,)), ((), ())), preferred_element_type=F32)
            if masked:
                sc = jnp.where(causal, sc, MASK_VALUE)
            m_old = m_ref[j]
            m_new = jnp.maximum(m_old, jnp.max(sc, axis=-1, keepdims=True))
            p = jnp.exp2(sc - jnp.concatenate([m_new] * (tq // LANES), axis=1))
            alpha = jnp.exp2(m_old - m_new)
            acc_ref[j] = alpha * acc_ref[j] + jnp.dot(p.astype(BF16), vc, preferred_element_type=F32)
            m_ref[j] = m_new

    def body(c, carry):
        update(c, False)
        return carry

    lax.fori_loop(0, i, body, 0)
    update(i, True)
    lane = lax.broadcasted_iota(jnp.int32, (1, LANES), 1)
    for a in range(heads // 2):
        even = acc_ref[2 * a]
        odd = acc_ref[2 * a + 1]
        o_even = even / even[:, MLA_V_DIM:MLA_V_DIM + 1]
        o_odd = odd / odd[:, 0:1]
        o_ref[:, a * LANES:(a + 1) * LANES] = jnp.where(lane < MLA_V_DIM, o_even, o_odd).astype(BF16)


def _ffn_kernel(x_ref, o_ref, part_ref, g1_ref, wo_ref, wout_ref, gffn_ref, w1_ref, w2_ref, gfin_ref,
                out_ref, *, final):
    y_mla = jnp.dot(o_ref[...], wo_ref[...], preferred_element_type=F32)
    merged = part_ref[...] + g1_ref[...].astype(F32) * y_mla
    x1 = x_ref[...] + jnp.dot(merged.astype(BF16), wout_ref[...], preferred_element_type=F32)
    hb = _rms(x1, gffn_ref[...]).astype(BF16)
    acc = x1
    for c in range(D_FF // FF_CHUNK):
        cs = slice(c * FF_CHUNK, (c + 1) * FF_CHUNK)
        f = jnp.dot(hb, w1_ref[:, cs], preferred_element_type=F32)
        f = jnp.square(jnp.maximum(f, 0.0)).astype(BF16)
        acc = acc + jnp.dot(f, w2_ref[cs, :], preferred_element_type=F32)
    if final:
        acc = _rms(acc, gfin_ref[...])
    out_ref[...] = acc


def _const_spec(shape):
    return pl.BlockSpec(shape, lambda *_: (0,) * len(shape), pipeline_mode=pl.Buffered(1))


def _params(n_axes):
    return pltpu.CompilerParams(dimension_semantics=("arbitrary",) * n_axes,
                                vmem_limit_bytes=VMEM_LIMIT_BYTES)


def _mix_call(x2, lw, tabs, batch, seq):
    ts = min(SEQ_TILE, seq)
    ns = seq // ts
    tok = batch * seq
    tile = lambda w: pl.BlockSpec((ts, w), lambda b, s: (b * ns + s, 0))
    tab = pl.BlockSpec((ts, LANES), lambda b, s: (s, 0))
    consts = [lw[n] for n in ("g_mix", "wsm", "wg", "bg")]
    consts2 = [lw[n] for n in ("wpg", "pscale", "wpp", "g_q", "wq", "g_kv", "wk", "wv",
                               "convw", "convb", "wax", "bax", "lam", "wlp")]
    in_specs = ([tile(D_MODEL)] + [_const_spec(c.shape) for c in consts] + [tab, tab, tab]
                + [_const_spec(c.shape) for c in consts2])
    out_shape = (jax.ShapeDtypeStruct((tok, MLA_HEADS * HEAD_PAD), BF16),
                 jax.ShapeDtypeStruct((tok, MLA_HEADS * HEAD_PAD), BF16),
                 jax.ShapeDtypeStruct((tok, MLA_HEADS * HEAD_PAD), BF16),
                 jax.ShapeDtypeStruct((tok, D_MODEL), F32),
                 jax.ShapeDtypeStruct((tok, D_MODEL), BF16))
    out_specs = (tile(MLA_HEADS * HEAD_PAD), tile(MLA_HEADS * HEAD_PAD), tile(MLA_HEADS * HEAD_PAD),
                 tile(D_MODEL), tile(D_MODEL))
    return pl.pallas_call(
        functools.partial(_mix_kernel, ts=ts),
        grid=(batch, ns),
        in_specs=in_specs,
        out_specs=out_specs,
        out_shape=out_shape,
        scratch_shapes=[pltpu.VMEM((HALO, POOL_WIDTH + LRU_WIDTH), F32),
                        pltpu.VMEM((8, LRU_WIDTH), F32)],
        compiler_params=_params(2),
        name="mix",
    )(x2, *consts, *tabs, *consts2)


def _attn_call(q, k, v, batch, seq):
    tq = min(ATTN_TILE, seq)
    nq = seq // tq
    tok = batch * seq
    hg = ATTN_HEADS_PER_STEP
    width = hg * HEAD_PAD
    return pl.pallas_call(
        functools.partial(_attn_kernel, tq=tq, heads=hg),
        grid=(batch, MLA_HEADS // hg, nq),
        in_specs=[pl.BlockSpec((tq, width), lambda b, g, i: (b * nq + i, g)),
                  pl.BlockSpec((seq, width), lambda b, g, i: (b, g)),
                  pl.BlockSpec((seq, width), lambda b, g, i: (b, g))],
        out_specs=pl.BlockSpec((tq, hg * MLA_V_DIM), lambda b, g, i: (b * nq + i, g)),
        out_shape=jax.ShapeDtypeStruct((tok, MLA_WIDTH), BF16),
        scratch_shapes=[pltpu.VMEM((hg, tq, LANES), F32), pltpu.VMEM((hg, tq, LANES), F32)],
        compiler_params=_params(3),
        name="attn",
    )(q, k, v)


def _ffn_call(x2, o, part, g1, lw, g_final, final):
    tok = x2.shape[0]
    tm = min(FFN_TILE, tok)
    tile = lambda w: pl.BlockSpec((tm, w), lambda t: (t, 0))
    consts = [lw[n] for n in ("wo", "wout", "g_ffn", "w1", "w2")] + [g_final]
    return pl.pallas_call(
        functools.partial(_ffn_kernel, final=final),
        grid=(tok // tm,),
        in_specs=[tile(D_MODEL), tile(MLA_WIDTH), tile(D_MODEL), tile(D_MODEL)]
                 + [_const_spec(c.shape) for c in consts],
        out_specs=tile(D_MODEL),
        out_shape=jax.ShapeDtypeStruct((tok, D_MODEL), F32),
        compiler_params=_params(1),
        name="ffn",
    )(x2, o, part, g1, *consts)


def _block_diag(w):
    depth, g, c, _ = w.shape
    eye = jnp.eye(g, dtype=w.dtype)
    return jnp.einsum("lgcd,gh->lgchd", w, eye).reshape(depth, g * c, g * c)


def _rope_tables(seq):
    pos = jnp.arange(seq, dtype=F32)
    inv = ROPE_THETA ** (-jnp.arange(0, MLA_ROPE_DIM, 2, dtype=F32) / MLA_ROPE_DIM)
    ang = pos[:, None] * inv[None, :]
    cos, sin = jnp.cos(ang), jnp.sin(ang)
    half = MLA_ROPE_DIM // 2
    ones = jnp.ones((seq, MLA_NOPE_DIM), F32)
    zh = jnp.zeros((seq, half), F32)
    tail = jnp.zeros((seq, LANES - MLA_QK_DIM), F32)
    zn = jnp.zeros((seq, MLA_NOPE_DIM), F32)
    cos_t = jnp.concatenate([ones, cos, cos, tail], axis=1)
    sa_t = jnp.concatenate([zn, zh, sin, tail], axis=1)
    sb_t = jnp.concatenate([zn, -sin, zh, tail], axis=1)
    return cos_t, sa_t, sb_t


def _prep_weights(w_in, w_pool_grp, pool_scale, w_pool_proj, g_mix, g_q, w_q_up, g_kv, w_kv_up, w_mla_o,
                  conv_w, conv_b, w_lru_a, b_lru_a, w_lru_x, b_lru_x, lru_lambda, w_lru_proj, b_gate,
                  w_out, g_ffn, w_ff1, w_ff2):
    depth = w_in.shape[0]
    row = lambda a: a[:, None, :]
    w_pool = w_in[:, :, :IN_POOL_END]
    w_qlat = w_in[:, :, IN_POOL_END:IN_Q_END]
    w_ckv = w_in[:, :, IN_Q_END:IN_Q_END + MLA_KV_RANK]
    w_kr = w_in[:, :, IN_Q_END + MLA_KV_RANK:IN_KV_END]
    w_lru = w_in[:, :, IN_KV_END:IN_LRU_END]
    w_gate = w_in[:, :, IN_LRU_END:]
    kr_blk = jnp.pad(w_kr, ((0, 0), (0, 0), (MLA_NOPE_DIM, LANES - MLA_QK_DIM)))
    wsm = jnp.concatenate([w_pool, w_lru, w_qlat, w_ckv, kr_blk], axis=2).astype(BF16)
    wq = jnp.pad(w_q_up.reshape(depth, MLA_Q_RANK, MLA_HEADS, MLA_QK_DIM),
                 ((0, 0), (0, 0), (0, 0), (0, HEAD_PAD - MLA_QK_DIM)))
    wq = wq.reshape(depth, MLA_Q_RANK, MLA_HEADS * HEAD_PAD).astype(BF16)
    wkv = w_kv_up.reshape(depth, MLA_KV_RANK, MLA_HEADS, MLA_NOPE_DIM + MLA_V_DIM)
    wk = jnp.pad(wkv[..., :MLA_NOPE_DIM], ((0, 0), (0, 0), (0, 0), (0, HEAD_PAD - MLA_NOPE_DIM)))
    wk = wk.reshape(depth, MLA_KV_RANK, MLA_HEADS * HEAD_PAD).astype(BF16)
    wv2 = wkv[..., MLA_NOPE_DIM:].reshape(depth, MLA_KV_RANK, MLA_HEADS // 2, 2, MLA_V_DIM)
    pad_v = HEAD_PAD - MLA_V_DIM
    wv_even = jnp.pad(wv2[:, :, :, 0], ((0, 0), (0, 0), (0, 0), (0, pad_v)))
    wv_odd = jnp.pad(wv2[:, :, :, 1], ((0, 0), (0, 0), (0, 0), (pad_v, 0)))
    wv = jnp.stack([wv_even, wv_odd], axis=3).reshape(depth, MLA_KV_RANK, MLA_HEADS * HEAD_PAD).astype(BF16)
    wax = jnp.concatenate([_block_diag(w_lru_a), _block_diag(w_lru_x)], axis=2).astype(BF16)
    return dict(
        g_mix=row(g_mix), wsm=wsm, wg=w_gate.astype(BF16), bg=row(b_gate),
        wpg=_block_diag(w_pool_grp).astype(BF16), pscale=row(pool_scale), wpp=w_pool_proj.astype(BF16),
        g_q=row(g_q), wq=wq, g_kv=row(g_kv), wk=wk, wv=wv,
        convw=conv_w, convb=row(conv_b), wax=wax,
        bax=row(jnp.concatenate([b_lru_a, b_lru_x], axis=1)), lam=row(lru_lambda),
        wlp=w_lru_proj.astype(BF16),
        wo=w_mla_o.astype(BF16), wout=w_out.astype(BF16), g_ffn=row(g_ffn),
        w1=w_ff1.astype(BF16), w2=w_ff2.astype(BF16))


def kernel(x, g_mix, w_in, w_pool_grp, pool_scale, w_pool_proj, g_q, w_q_up, g_kv, w_kv_up, w_mla_o, conv_w, conv_b, w_lru_a, b_lru_a, w_lru_x, b_lru_x, lru_lambda, w_lru_proj, b_gate, w_out, g_ffn, w_ff1, w_ff2, g_final):
    batch, seq, d = x.shape
    assert d == D_MODEL and seq % min(SEQ_TILE, seq) == 0 and seq >= HALO
    depth = w_in.shape[0]
    weights = _prep_weights(w_in, w_pool_grp, pool_scale, w_pool_proj, g_mix, g_q, w_q_up, g_kv, w_kv_up,
                            w_mla_o, conv_w, conv_b, w_lru_a, b_lru_a, w_lru_x, b_lru_x, lru_lambda,
                            w_lru_proj, b_gate, w_out, g_ffn, w_ff1, w_ff2)
    tabs = _rope_tables(seq)
    gfin = g_final[None, :]
    x2 = x.reshape(batch * seq, d)
    for l in range(depth):
        lw = {n: w[l] for n, w in weights.items()}
        q, k, v, part, g1 = _mix_call(x2, lw, tabs, batch, seq)
        o = _attn_call(q, k, v, batch, seq)
        x2 = _ffn_call(x2, o, part, g1, lw, gfin, final=(l == depth - 1))
    return x2.reshape(batch, seq, d)
```

```python
import functools

import jax
import jax.numpy as jnp
from jax import lax
from jax.experimental import pallas as pl
from jax.experimental.pallas import tpu as pltpu

F32 = jnp.float32
BF16 = jnp.bfloat16

D_MODEL = 1024
DEPTH = 4
POOL_GROUPS = 4
POOL_GROUP_DIM = 64
POOL_WIDTH = 256
POOL_WINDOWS = (2, 4, 8, 16)
MLA_HEADS = 8
MLA_NOPE_DIM = 64
MLA_ROPE_DIM = 32
MLA_V_DIM = 64
MLA_Q_RANK = 384
MLA_KV_RANK = 256
MLA_QK_DIM = MLA_NOPE_DIM + MLA_ROPE_DIM
MLA_WIDTH = MLA_HEADS * MLA_V_DIM
ROPE_THETA = 10000.0
LRU_WIDTH = 256
CONV_WIDTH = 4
LRU_C = 8.0
D_FF = 4 * D_MODEL
EPS = 1e-6
LOG2_E = 1.4426950408889634

IN_POOL_END = POOL_WIDTH
IN_Q_END = IN_POOL_END + MLA_Q_RANK
IN_KV_END = IN_Q_END + MLA_KV_RANK + MLA_ROPE_DIM
IN_LRU_END = IN_KV_END + LRU_WIDTH

LANES = 128
SUBLANES = 8
HEAD_PAD = LANES
HALO = 16
SMALL_COLS = POOL_WIDTH + LRU_WIDTH + MLA_Q_RANK + MLA_KV_RANK + LANES
VMEM_LIMIT_BYTES = 56 * 1024 * 1024
MASK_VALUE = -1e30

SEQ_TILE = 512
ATTN_TILE = 512
ATTN_HEADS_PER_STEP = 4
FFN_TILE = 512
FF_CHUNK = 1024


def _rms(x, g):
    return x * lax.rsqrt(jnp.mean(x * x, axis=-1, keepdims=True) + EPS) * g


def _sigmoid_of_twice(half_x):
    return 0.5 * jnp.tanh(half_x) + 0.5


def _rope(blk, cos, sin):
    return blk * cos + pltpu.roll(blk, LANES // 2, 1) * sin


def _mix_kernel(x_ref, gmix_ref, wsm_ref, wg_ref, bg_ref, cos_ref, sin_ref,
                wpg_ref, pscale_ref, wpp_ref, gq_ref, wq_ref, gkv_ref, wk_ref, wv_ref,
                convw_ref, convb_ref, wax_ref, bax_ref, lam_ref, wlp_ref,
                q_ref, k_ref, v_ref, part_ref, g1_ref,
                halo_ref, h_ref, *, ts):
    s = pl.program_id(1)

    @pl.when(s == 0)
    def _():
        halo_ref[...] = jnp.zeros_like(halo_ref)
        h_ref[...] = jnp.zeros_like(h_ref)

    hb = _rms(x_ref[...], gmix_ref[...]).astype(BF16)
    small = jnp.dot(hb, wsm_ref[...], preferred_element_type=F32)

    u2 = small[:, :POOL_WIDTH + LRU_WIDTH]
    ext = jnp.concatenate([halo_ref[...], u2], axis=0)
    halo_ref[...] = u2[ts - HALO:, :]
    ep = ext[:, :POOL_WIDTH]
    el = ext[:, POOL_WIDTH:]

    s2 = ep + pltpu.roll(ep, 1, 0)
    s4 = s2 + pltpu.roll(s2, 2, 0)
    s8 = s4 + pltpu.roll(s4, 4, 0)
    s16 = s8 + pltpu.roll(s8, 8, 0)
    lane = lax.broadcasted_iota(jnp.int32, (1, POOL_WIDTH), 1)
    grp = lane // POOL_GROUP_DIM
    wsum = jnp.where(grp == 0, s2, jnp.where(grp == 1, s4, jnp.where(grp == 2, s8, s16)))[HALO:]
    win = jnp.where(grp == 0, float(POOL_WINDOWS[0]),
                    jnp.where(grp == 1, float(POOL_WINDOWS[1]),
                              jnp.where(grp == 2, float(POOL_WINDOWS[2]), float(POOL_WINDOWS[3]))))
    row = lax.broadcasted_iota(jnp.int32, (ts, 1), 0)
    tpos = (s * ts + row + 1).astype(F32)
    cnt = jnp.minimum(tpos, win)
    mixed = wsum / cnt - u2[:, :POOL_WIDTH]
    yp = jnp.dot(mixed.astype(BF16), wpg_ref[...], preferred_element_type=F32) * pscale_ref[...]
    y_pool = jnp.dot(yp.astype(BF16), wpp_ref[...], preferred_element_type=F32)

    cw = convw_ref[...]
    uc = (cw[3:4, :] * el + cw[2:3, :] * pltpu.roll(el, 1, 0)
          + cw[1:2, :] * pltpu.roll(el, 2, 0) + cw[0:1, :] * pltpu.roll(el, 3, 0))
    uc = uc[HALO:] + convb_ref[...]
    ri = _sigmoid_of_twice(jnp.dot(uc.astype(BF16), wax_ref[...], preferred_element_type=F32) + bax_ref[...])
    r = ri[:, :LRU_WIDTH]
    ig = ri[:, LRU_WIDTH:]
    nlam = -lam_ref[...]
    softplus = jnp.maximum(nlam, 0.0) + jnp.log1p(jnp.exp(-jnp.abs(nlam)))
    log_a = (-LRU_C) * r * softplus
    a_cum = jnp.exp(log_a)
    b_cum = jnp.sqrt(-jnp.tanh(log_a) * (a_cum * a_cum + 1.0)) * (ig * uc)
    sub = lax.broadcasted_iota(jnp.int32, (1, SUBLANES, 1), 1)
    a3 = a_cum.reshape(ts // SUBLANES, SUBLANES, LRU_WIDTH)
    b3 = b_cum.reshape(ts // SUBLANES, SUBLANES, LRU_WIDTH)
    k = 1
    while k < SUBLANES:
        keep = sub >= k
        a_sh = jnp.where(keep, pltpu.roll(a3, k, 1), 1.0)
        b_sh = jnp.where(keep, pltpu.roll(b3, k, 1), 0.0)
        b3 = a3 * b_sh + b3
        a3 = a3 * a_sh
        k *= 2
    carry = h_ref[...]
    tiles = []
    for t in range(ts // SUBLANES):
        h_t = a3[t] * carry + b3[t]
        tiles.append(h_t)
        carry = jnp.broadcast_to(h_t[SUBLANES - 1:, :], h_t.shape)
    hh = jnp.concatenate(tiles, axis=0)
    h_ref[...] = carry
    y_lru = jnp.dot(hh.astype(BF16), wlp_ref[...], preferred_element_type=F32)

    def gate(j):
        z = jnp.dot(hb, wg_ref[:, j * D_MODEL:(j + 1) * D_MODEL], preferred_element_type=F32)
        return _sigmoid_of_twice(z + bg_ref[:, j * D_MODEL:(j + 1) * D_MODEL])

    part_ref[...] = gate(0) * y_pool + gate(2) * y_lru
    g1_ref[...] = gate(1).astype(BF16)

    cos = cos_ref[...]
    sin = sin_ref[...]
    q0 = POOL_WIDTH + LRU_WIDTH
    c0 = q0 + MLA_Q_RANK
    r0 = c0 + MLA_KV_RANK
    qn = _rms(small[:, q0:c0], gq_ref[...]) * (MLA_QK_DIM ** -0.5 * LOG2_E)
    qf = jnp.dot(qn.astype(BF16), wq_ref[...], preferred_element_type=F32)
    cn = _rms(small[:, c0:r0], gkv_ref[...]).astype(BF16)
    kf = jnp.dot(cn, wk_ref[...], preferred_element_type=F32)
    kr = _rope(small[:, r0:r0 + LANES], cos, sin)
    for hd in range(MLA_HEADS):
        sl = slice(hd * HEAD_PAD, (hd + 1) * HEAD_PAD)
        q_ref[:, sl] = _rope(qf[:, sl], cos, sin).astype(BF16)
        k_ref[:, sl] = (kf[:, sl] + kr).astype(BF16)
    vlane = lax.broadcasted_iota(jnp.int32, (1, MLA_HEADS * HEAD_PAD), 1)
    vpos = vlane % (2 * HEAD_PAD)
    vone = jnp.where((vpos == MLA_V_DIM) | (vpos == HEAD_PAD), 1.0, 0.0)
    v_ref[...] = (jnp.dot(cn, wv_ref[...], preferred_element_type=F32) + vone).astype(BF16)


def _attn_kernel(q_ref, k_ref, v_ref, o_ref, m_ref, alpha0_ref, p0_ref, alpha1_ref, p1_ref, acc_ref,
                 *, tq, heads):
    i = pl.program_id(2)
    bufs = ((p0_ref, alpha0_ref), (p1_ref, alpha1_ref))
    row = lax.broadcasted_iota(jnp.int32, (tq, tq), 0)
    col = lax.broadcasted_iota(jnp.int32, (tq, tq), 1)
    causal = col <= row
    for j in range(heads):
        m_ref[j] = jnp.full((tq, LANES), MASK_VALUE, F32)
    for a in range(heads // 2):
        acc_ref[a] = jnp.zeros((tq, 2 * LANES), F32)
    zero_v = jnp.zeros((tq, HEAD_PAD), BF16)

    def scores(c, masked, buf):
        p_ref, alpha_ref = bufs[buf]
        start = pl.multiple_of(c * tq, tq)
        for j in range(heads):
            hs = slice(j * HEAD_PAD, (j + 1) * HEAD_PAD)
            a, half = divmod(j, 2)
            kc = k_ref[pl.ds(start, tq), hs]
            sc = lax.dot_general(q_ref[:, hs], kc, (((1,), (1,)), ((), ())), preferred_element_type=F32)
            if masked:
                sc = jnp.where(causal, sc, MASK_VALUE)
            m_old = m_ref[j]
            m_new = jnp.maximum(m_old, jnp.max(sc, axis=-1, keepdims=True))
            p = jnp.exp2(sc - jnp.concatenate([m_new] * (tq // LANES), axis=1))
            p_ref[a, :, half * tq:(half + 1) * tq] = p.astype(BF16)
            alpha_ref[a, :, half * LANES:(half + 1) * LANES] = jnp.exp2(m_old - m_new)
            m_ref[j] = m_new

    def accumulate(c, buf):
        p_ref, alpha_ref = bufs[buf]
        start = pl.multiple_of(c * tq, tq)
        for a in range(heads // 2):
            v_even = v_ref[pl.ds(start, tq), (2 * a) * HEAD_PAD:(2 * a + 1) * HEAD_PAD]
            v_odd = v_ref[pl.ds(start, tq), (2 * a + 1) * HEAD_PAD:(2 * a + 2) * HEAD_PAD]
            v2 = jnp.concatenate([jnp.concatenate([v_even, zero_v], axis=1),
                                  jnp.concatenate([zero_v, v_odd], axis=1)], axis=0)
            acc_ref[a] = alpha_ref[a] * acc_ref[a] + jnp.dot(p_ref[a], v2, preferred_element_type=F32)

    scores(i, True, 0)

    def body(u, carry):
        scores(2 * u, False, 1)
        accumulate(jnp.where(u == 0, i, 2 * u - 1), 0)
        scores(2 * u + 1, False, 0)
        accumulate(2 * u, 1)
        return carry

    lax.fori_loop(0, i // 2, body, 0)

    @pl.when(i % 2 == 1)
    def _():
        scores(i - 1, False, 1)
        accumulate(jnp.where(i == 1, i, i - 2), 0)
        accumulate(i - 1, 1)

    @pl.when(i % 2 == 0)
    def _():
        accumulate(jnp.maximum(i - 1, 0), 0)

    lane = lax.broadcasted_iota(jnp.int32, (1, LANES), 1)
    for a in range(heads // 2):
        even = acc_ref[a][:, :LANES]
        odd = acc_ref[a][:, LANES:]
        o_even = even / even[:, MLA_V_DIM:MLA_V_DIM + 1]
        o_odd = odd / odd[:, 0:1]
        o_ref[:, a * LANES:(a + 1) * LANES] = jnp.where(lane < MLA_V_DIM, o_even, o_odd).astype(BF16)


def _ffn_kernel(x_ref, o_ref, part_ref, g1_ref, wo_ref, wout_ref, gffn_ref, w1_ref, w2_ref, gfin_ref,
                out_ref, *, final):
    y_mla = jnp.dot(o_ref[...], wo_ref[...], preferred_element_type=F32)
    merged = part_ref[...] + g1_ref[...].astype(F32) * y_mla
    x1 = x_ref[...] + jnp.dot(merged.astype(BF16), wout_ref[...], preferred_element_type=F32)
    hb = _rms(x1, gffn_ref[...]).astype(BF16)
    acc = x1
    for c in range(D_FF // FF_CHUNK):
        cs = slice(c * FF_CHUNK, (c + 1) * FF_CHUNK)
        f = jnp.dot(hb, w1_ref[:, cs], preferred_element_type=F32)
        f = jnp.square(jnp.maximum(f, 0.0)).astype(BF16)
        acc = acc + jnp.dot(f, w2_ref[cs, :], preferred_element_type=F32)
    if final:
        acc = _rms(acc, gfin_ref[...])
    out_ref[...] = acc


def _const_spec(shape):
    return pl.BlockSpec(shape, lambda *_: (0,) * len(shape), pipeline_mode=pl.Buffered(1))


def _params(n_axes):
    return pltpu.CompilerParams(dimension_semantics=("arbitrary",) * n_axes,
                                vmem_limit_bytes=VMEM_LIMIT_BYTES)


def _mix_call(x2, lw, tabs, batch, seq):
    ts = min(SEQ_TILE, seq)
    ns = seq // ts
    tok = batch * seq
    tile = lambda w: pl.BlockSpec((ts, w), lambda b, s: (b * ns + s, 0))
    tab = pl.BlockSpec((ts, LANES), lambda b, s: (s, 0))
    consts = [lw[n] for n in ("g_mix", "wsm", "wg", "bg")]
    consts2 = [lw[n] for n in ("wpg", "pscale", "wpp", "g_q", "wq", "g_kv", "wk", "wv",
                               "convw", "convb", "wax", "bax", "lam", "wlp")]
    in_specs = ([tile(D_MODEL)] + [_const_spec(c.shape) for c in consts] + [tab, tab]
                + [_const_spec(c.shape) for c in consts2])
    out_shape = (jax.ShapeDtypeStruct((tok, MLA_HEADS * HEAD_PAD), BF16),
                 jax.ShapeDtypeStruct((tok, MLA_HEADS * HEAD_PAD), BF16),
                 jax.ShapeDtypeStruct((tok, MLA_HEADS * HEAD_PAD), BF16),
                 jax.ShapeDtypeStruct((tok, D_MODEL), F32),
                 jax.ShapeDtypeStruct((tok, D_MODEL), BF16))
    out_specs = (tile(MLA_HEADS * HEAD_PAD), tile(MLA_HEADS * HEAD_PAD), tile(MLA_HEADS * HEAD_PAD),
                 tile(D_MODEL), tile(D_MODEL))
    return pl.pallas_call(
        functools.partial(_mix_kernel, ts=ts),
        grid=(batch, ns),
        in_specs=in_specs,
        out_specs=out_specs,
        out_shape=out_shape,
        scratch_shapes=[pltpu.VMEM((HALO, POOL_WIDTH + LRU_WIDTH), F32),
                        pltpu.VMEM((8, LRU_WIDTH), F32)],
        compiler_params=_params(2),
        name="mix",
    )(x2, *consts, *tabs, *consts2)


def _attn_call(q, k, v, batch, seq):
    tq = min(ATTN_TILE, seq)
    nq = seq // tq
    tok = batch * seq
    hg = ATTN_HEADS_PER_STEP
    width = hg * HEAD_PAD
    return pl.pallas_call(
        functools.partial(_attn_kernel, tq=tq, heads=hg),
        grid=(batch, MLA_HEADS // hg, nq),
        in_specs=[pl.BlockSpec((tq, width), lambda b, g, i: (b * nq + i, g)),
                  pl.BlockSpec((seq, width), lambda b, g, i: (b, g)),
                  pl.BlockSpec((seq, width), lambda b, g, i: (b, g))],
        out_specs=pl.BlockSpec((tq, hg * MLA_V_DIM), lambda b, g, i: (b * nq + i, g)),
        out_shape=jax.ShapeDtypeStruct((tok, MLA_WIDTH), BF16),
        scratch_shapes=[pltpu.VMEM((hg, tq, LANES), F32),
                        pltpu.VMEM((hg // 2, tq, 2 * LANES), F32),
                        pltpu.VMEM((hg // 2, tq, 2 * tq), BF16),
                        pltpu.VMEM((hg // 2, tq, 2 * LANES), F32),
                        pltpu.VMEM((hg // 2, tq, 2 * tq), BF16),
                        pltpu.VMEM((hg // 2, tq, 2 * LANES), F32)],
        compiler_params=_params(3),
        name="attn",
    )(q, k, v)


def _ffn_call(x2, o, part, g1, lw, g_final, final):
    tok = x2.shape[0]
    tm = min(FFN_TILE, tok)
    tile = lambda w: pl.BlockSpec((tm, w), lambda t: (t, 0))
    consts = [lw[n] for n in ("wo", "wout", "g_ffn", "w1", "w2")] + [g_final]
    return pl.pallas_call(
        functools.partial(_ffn_kernel, final=final),
        grid=(tok // tm,),
        in_specs=[tile(D_MODEL), tile(MLA_WIDTH), tile(D_MODEL), tile(D_MODEL)]
                 + [_const_spec(c.shape) for c in consts],
        out_specs=tile(D_MODEL),
        out_shape=jax.ShapeDtypeStruct((tok, D_MODEL), F32),
        compiler_params=_params(1),
        name="ffn",
    )(x2, o, part, g1, *consts)


def _block_diag(w):
    depth, g, c, _ = w.shape
    eye = jnp.eye(g, dtype=w.dtype)
    return jnp.einsum("lgcd,gh->lgchd", w, eye).reshape(depth, g * c, g * c)


def _rope_tables(seq):
    pos = jnp.arange(seq, dtype=F32)
    inv = ROPE_THETA ** (-jnp.arange(0, MLA_ROPE_DIM, 2, dtype=F32) / MLA_ROPE_DIM)
    ang = pos[:, None] * inv[None, :]
    cos, sin = jnp.cos(ang), jnp.sin(ang)
    cos_t = _head_layout(jnp.ones((seq, MLA_NOPE_DIM), F32), jnp.concatenate([cos, cos], axis=1))
    sin_t = _head_layout(jnp.zeros((seq, MLA_NOPE_DIM), F32), jnp.concatenate([-sin, sin], axis=1))
    return cos_t, sin_t


def _head_layout(nope, rope):
    half = MLA_ROPE_DIM // 2
    split = LANES // 2 - half
    tail = jnp.zeros(nope.shape[:-1] + (HEAD_PAD - MLA_QK_DIM,), nope.dtype)
    return jnp.concatenate([rope[..., :half], nope[..., :split], rope[..., half:], nope[..., split:], tail],
                           axis=-1)


def _prep_weights(w_in, w_pool_grp, pool_scale, w_pool_proj, g_mix, g_q, w_q_up, g_kv, w_kv_up, w_mla_o,
                  conv_w, conv_b, w_lru_a, b_lru_a, w_lru_x, b_lru_x, lru_lambda, w_lru_proj, b_gate,
                  w_out, g_ffn, w_ff1, w_ff2):
    depth = w_in.shape[0]
    row = lambda a: a[:, None, :]
    w_pool = w_in[:, :, :IN_POOL_END]
    w_qlat = w_in[:, :, IN_POOL_END:IN_Q_END]
    w_ckv = w_in[:, :, IN_Q_END:IN_Q_END + MLA_KV_RANK]
    w_kr = w_in[:, :, IN_Q_END + MLA_KV_RANK:IN_KV_END]
    w_lru = w_in[:, :, IN_KV_END:IN_LRU_END]
    w_gate = w_in[:, :, IN_LRU_END:]
    kr_blk = _head_layout(jnp.zeros(w_kr.shape[:-1] + (MLA_NOPE_DIM,), w_kr.dtype), w_kr)
    wsm = jnp.concatenate([w_pool, w_lru, w_qlat, w_ckv, kr_blk], axis=2).astype(BF16)
    wq4 = w_q_up.reshape(depth, MLA_Q_RANK, MLA_HEADS, MLA_QK_DIM)
    wq = _head_layout(wq4[..., :MLA_NOPE_DIM], wq4[..., MLA_NOPE_DIM:])
    wq = wq.reshape(depth, MLA_Q_RANK, MLA_HEADS * HEAD_PAD).astype(BF16)
    wkv = w_kv_up.reshape(depth, MLA_KV_RANK, MLA_HEADS, MLA_NOPE_DIM + MLA_V_DIM)
    wk = _head_layout(wkv[..., :MLA_NOPE_DIM], jnp.zeros(wkv.shape[:-1] + (MLA_ROPE_DIM,), wkv.dtype))
    wk = wk.reshape(depth, MLA_KV_RANK, MLA_HEADS * HEAD_PAD).astype(BF16)
    wv2 = wkv[..., MLA_NOPE_DIM:].reshape(depth, MLA_KV_RANK, MLA_HEADS // 2, 2, MLA_V_DIM)
    pad_v = HEAD_PAD - MLA_V_DIM
    wv_even = jnp.pad(wv2[:, :, :, 0], ((0, 0), (0, 0), (0, 0), (0, pad_v)))
    wv_odd = jnp.pad(wv2[:, :, :, 1], ((0, 0), (0, 0), (0, 0), (pad_v, 0)))
    wv = jnp.stack([wv_even, wv_odd], axis=3).reshape(depth, MLA_KV_RANK, MLA_HEADS * HEAD_PAD).astype(BF16)
    wax = (0.5 * jnp.concatenate([_block_diag(w_lru_a), _block_diag(w_lru_x)], axis=2)).astype(BF16)
    return dict(
        g_mix=row(g_mix), wsm=wsm, wg=(0.5 * w_gate).astype(BF16), bg=row(0.5 * b_gate),
        wpg=_block_diag(w_pool_grp).astype(BF16), pscale=row(pool_scale), wpp=w_pool_proj.astype(BF16),
        g_q=row(g_q), wq=wq, g_kv=row(g_kv), wk=wk, wv=wv,
        convw=conv_w, convb=row(conv_b), wax=wax,
        bax=row(0.5 * jnp.concatenate([b_lru_a, b_lru_x], axis=1)), lam=row(lru_lambda),
        wlp=w_lru_proj.astype(BF16),
        wo=w_mla_o.astype(BF16), wout=w_out.astype(BF16), g_ffn=row(g_ffn),
        w1=w_ff1.astype(BF16), w2=w_ff2.astype(BF16))


def kernel(x, g_mix, w_in, w_pool_grp, pool_scale, w_pool_proj, g_q, w_q_up, g_kv, w_kv_up, w_mla_o, conv_w, conv_b, w_lru_a, b_lru_a, w_lru_x, b_lru_x, lru_lambda, w_lru_proj, b_gate, w_out, g_ffn, w_ff1, w_ff2, g_final):
    batch, seq, d = x.shape
    assert d == D_MODEL and seq % min(SEQ_TILE, seq) == 0 and seq >= HALO
    depth = w_in.shape[0]
    weights = _prep_weights(w_in, w_pool_grp, pool_scale, w_pool_proj, g_mix, g_q, w_q_up, g_kv, w_kv_up,
                            w_mla_o, conv_w, conv_b, w_lru_a, b_lru_a, w_lru_x, b_lru_x, lru_lambda,
                            w_lru_proj, b_gate, w_out, g_ffn, w_ff1, w_ff2)
    tabs = _rope_tables(seq)
    gfin = g_final[None, :]
    x2 = x.reshape(batch * seq, d)
    for l in range(depth):
        lw = {n: w[l] for n, w in weights.items()}
        q, k, v, part, g1 = _mix_call(x2, lw, tabs, batch, seq)
        o = _attn_call(q, k, v, batch, seq)
        x2 = _ffn_call(x2, o, part, g1, lw, gfin, final=(l == depth - 1))
    return x2.reshape(batch, seq, d)
```

```python
import functools

import jax
import jax.numpy as jnp
from jax import lax
from jax.experimental import pallas as pl
from jax.experimental.pallas import tpu as pltpu

F32 = jnp.float32
BF16 = jnp.bfloat16

D_MODEL = 1024
DEPTH = 4
POOL_GROUPS = 4
POOL_GROUP_DIM = 64
POOL_WIDTH = 256
POOL_WINDOWS = (2, 4, 8, 16)
MLA_HEADS = 8
MLA_NOPE_DIM = 64
MLA_ROPE_DIM = 32
MLA_V_DIM = 64
MLA_Q_RANK = 384
MLA_KV_RANK = 256
MLA_QK_DIM = MLA_NOPE_DIM + MLA_ROPE_DIM
MLA_WIDTH = MLA_HEADS * MLA_V_DIM
ROPE_THETA = 10000.0
LRU_WIDTH = 256
CONV_WIDTH = 4
LRU_C = 8.0
D_FF = 4 * D_MODEL
EPS = 1e-6
LOG2_E = 1.4426950408889634

IN_POOL_END = POOL_WIDTH
IN_Q_END = IN_POOL_END + MLA_Q_RANK
IN_KV_END = IN_Q_END + MLA_KV_RANK + MLA_ROPE_DIM
IN_LRU_END = IN_KV_END + LRU_WIDTH

LANES = 128
SUBLANES = 8
HEAD_PAD = LANES
HALO = 16
SMALL_COLS = POOL_WIDTH + LRU_WIDTH + MLA_Q_RANK + MLA_KV_RANK + LANES
VMEM_LIMIT_BYTES = 56 * 1024 * 1024
MASK_VALUE = -1e30

SEQ_TILE = 512
ATTN_TILE = 512
ATTN_HEADS_PER_STEP = 4
FFN_TILE = 512
FF_CHUNK = 1024


def _rms(x, g):
    return x * lax.rsqrt(jnp.mean(x * x, axis=-1, keepdims=True) + EPS) * g


def _sigmoid_of_twice(half_x):
    return 0.5 * jnp.tanh(half_x) + 0.5


def _rope(blk, cos, sin):
    return blk * cos + pltpu.roll(blk, LANES // 2, 1) * sin


def _mix_kernel(x_ref, gmix_ref, wsm_ref, wg_ref, bg_ref, cos_ref, sin_ref,
                wpg_ref, pscale_ref, wpp_ref, gq_ref, wq_ref, gkv_ref, wk_ref, wv_ref,
                convw_ref, convb_ref, wax_ref, bax_ref, lam_ref, wlp_ref,
                q_ref, k_ref, v_ref, part_ref, g1_ref,
                halo_ref, h_ref, *, ts):
    s = pl.program_id(1)

    @pl.when(s == 0)
    def _():
        halo_ref[...] = jnp.zeros_like(halo_ref)
        h_ref[...] = jnp.zeros_like(h_ref)

    hb = _rms(x_ref[...], gmix_ref[...]).astype(BF16)
    small = jnp.dot(hb, wsm_ref[...], preferred_element_type=F32)

    def gate_logit(j):
        return jnp.dot(hb, wg_ref[:, j * D_MODEL:(j + 1) * D_MODEL], preferred_element_type=F32)

    def gate(j, z):
        return _sigmoid_of_twice(z + bg_ref[:, j * D_MODEL:(j + 1) * D_MODEL])

    u2 = small[:, :POOL_WIDTH + LRU_WIDTH]
    ext = jnp.concatenate([halo_ref[...], u2], axis=0)
    halo_ref[...] = u2[ts - HALO:, :]
    ep = ext[:, :POOL_WIDTH]
    el = ext[:, POOL_WIDTH:]
    row = lax.broadcasted_iota(jnp.int32, (ts, 1), 0)

    cw = convw_ref[...]
    uc = (cw[3:4, :] * el + cw[2:3, :] * pltpu.roll(el, 1, 0)
          + cw[1:2, :] * pltpu.roll(el, 2, 0) + cw[0:1, :] * pltpu.roll(el, 3, 0))
    uc = uc[HALO:] + convb_ref[...]
    ri = _sigmoid_of_twice(jnp.dot(uc.astype(BF16), wax_ref[...], preferred_element_type=F32) + bax_ref[...])
    z_lru = gate_logit(2)
    g1_ref[...] = gate(1, gate_logit(1)).astype(BF16)
    z_pool = gate_logit(0)
    r = ri[:, :LRU_WIDTH]
    ig = ri[:, LRU_WIDTH:]
    nlam = -lam_ref[...]
    softplus = jnp.maximum(nlam, 0.0) + jnp.log1p(jnp.exp(-jnp.abs(nlam)))
    log_a = (-LRU_C) * r * softplus
    a_cum = jnp.exp(log_a)
    b_cum = jnp.sqrt(-jnp.tanh(log_a) * (a_cum * a_cum + 1.0)) * (ig * uc)
    sub = lax.broadcasted_iota(jnp.int32, (1, SUBLANES, 1), 1)
    a3 = a_cum.reshape(ts // SUBLANES, SUBLANES, LRU_WIDTH)
    b3 = b_cum.reshape(ts // SUBLANES, SUBLANES, LRU_WIDTH)
    k = 1
    while k < SUBLANES:
        keep = sub >= k
        a_sh = jnp.where(keep, pltpu.roll(a3, k, 1), 1.0)
        b_sh = jnp.where(keep, pltpu.roll(b3, k, 1), 0.0)
        b3 = a3 * b_sh + b3
        a3 = a3 * a_sh
        k *= 2
    carry = h_ref[...]
    tiles = []
    for t in range(ts // SUBLANES):
        h_t = a3[t] * carry + b3[t]
        tiles.append(h_t)
        carry = jnp.broadcast_to(h_t[SUBLANES - 1:, :], h_t.shape)
    hh = jnp.concatenate(tiles, axis=0)
    h_ref[...] = carry
    y_lru = jnp.dot(hh.astype(BF16), wlp_ref[...], preferred_element_type=F32)

    part_lru = gate(2, z_lru) * y_lru

    s2 = ep + pltpu.roll(ep, 1, 0)
    s4 = s2 + pltpu.roll(s2, 2, 0)
    s8 = s4 + pltpu.roll(s4, 4, 0)
    s16 = s8 + pltpu.roll(s8, 8, 0)
    lane = lax.broadcasted_iota(jnp.int32, (1, POOL_WIDTH), 1)
    grp = lane // POOL_GROUP_DIM
    wsum = jnp.where(grp == 0, s2, jnp.where(grp == 1, s4, jnp.where(grp == 2, s8, s16)))[HALO:]
    win = jnp.where(grp == 0, float(POOL_WINDOWS[0]),
                    jnp.where(grp == 1, float(POOL_WINDOWS[1]),
                              jnp.where(grp == 2, float(POOL_WINDOWS[2]), float(POOL_WINDOWS[3]))))
    tpos = (s * ts + row + 1).astype(F32)
    cnt = jnp.minimum(tpos, win)
    mixed = wsum / cnt - u2[:, :POOL_WIDTH]
    yp = jnp.dot(mixed.astype(BF16), wpg_ref[...], preferred_element_type=F32) * pscale_ref[...]
    y_pool = jnp.dot(yp.astype(BF16), wpp_ref[...], preferred_element_type=F32)
    part_ref[...] = gate(0, z_pool) * y_pool + part_lru

    cos = cos_ref[...]
    sin = sin_ref[...]
    q0 = POOL_WIDTH + LRU_WIDTH
    c0 = q0 + MLA_Q_RANK
    r0 = c0 + MLA_KV_RANK
    qn = _rms(small[:, q0:c0], gq_ref[...]) * (MLA_QK_DIM ** -0.5 * LOG2_E)
    qf = jnp.dot(qn.astype(BF16), wq_ref[...], preferred_element_type=F32)
    cn = _rms(small[:, c0:r0], gkv_ref[...]).astype(BF16)
    kf = jnp.dot(cn, wk_ref[...], preferred_element_type=F32)
    kr = _rope(small[:, r0:r0 + LANES], cos, sin)
    for hd in range(MLA_HEADS):
        sl = slice(hd * HEAD_PAD, (hd + 1) * HEAD_PAD)
        q_ref[:, sl] = _rope(qf[:, sl], cos, sin).astype(BF16)
        k_ref[:, sl] = (kf[:, sl] + kr).astype(BF16)
    vlane = lax.broadcasted_iota(jnp.int32, (1, MLA_HEADS * HEAD_PAD), 1)
    vpos = vlane % (2 * HEAD_PAD)
    vone = jnp.where((vpos == MLA_V_DIM) | (vpos == HEAD_PAD), 1.0, 0.0)
    v_ref[...] = (jnp.dot(cn, wv_ref[...], preferred_element_type=F32) + vone).astype(BF16)


def _attn_kernel(q_ref, k_ref, v_ref, o_ref, m_ref, alpha0_ref, p0_ref, alpha1_ref, p1_ref, acc_ref,
                 *, tq, heads):
    i = pl.program_id(2)
    bufs = ((p0_ref, alpha0_ref), (p1_ref, alpha1_ref))
    row = lax.broadcasted_iota(jnp.int32, (tq, tq), 0)
    col = lax.broadcasted_iota(jnp.int32, (tq, tq), 1)
    causal = col <= row
    for j in range(heads):
        m_ref[j] = jnp.full((tq, LANES), MASK_VALUE, F32)
    for a in range(heads // 2):
        acc_ref[a] = jnp.zeros((tq, 2 * LANES), F32)
    zero_v = jnp.zeros((tq, HEAD_PAD), BF16)

    def scores(c, masked, buf):
        p_ref, alpha_ref = bufs[buf]
        start = pl.multiple_of(c * tq, tq)
        for j in range(heads):
            hs = slice(j * HEAD_PAD, (j + 1) * HEAD_PAD)
            a, half = divmod(j, 2)
            kc = k_ref[pl.ds(start, tq), hs]
            sc = lax.dot_general(q_ref[:, hs], kc, (((1,), (1,)), ((), ())), preferred_element_type=F32)
            if masked:
                sc = jnp.where(causal, sc, MASK_VALUE)
            m_old = m_ref[j]
            m_new = jnp.maximum(m_old, jnp.max(sc, axis=-1, keepdims=True))
            p = jnp.exp2(sc - jnp.concatenate([m_new] * (tq // LANES), axis=1))
            p_ref[a, :, half * tq:(half + 1) * tq] = p.astype(BF16)
            alpha_ref[a, :, half * LANES:(half + 1) * LANES] = jnp.exp2(m_old - m_new)
            m_ref[j] = m_new

    def accumulate(c, buf):
        p_ref, alpha_ref = bufs[buf]
        start = pl.multiple_of(c * tq, tq)
        for a in range(heads // 2):
            v_even = v_ref[pl.ds(start, tq), (2 * a) * HEAD_PAD:(2 * a + 1) * HEAD_PAD]
            v_odd = v_ref[pl.ds(start, tq), (2 * a + 1) * HEAD_PAD:(2 * a + 2) * HEAD_PAD]
            v2 = jnp.concatenate([jnp.concatenate([v_even, zero_v], axis=1),
                                  jnp.concatenate([zero_v, v_odd], axis=1)], axis=0)
            acc_ref[a] = alpha_ref[a] * acc_ref[a] + jnp.dot(p_ref[a], v2, preferred_element_type=F32)

    scores(i, True, 0)

    def body(u, carry):
        scores(2 * u, False, 1)
        accumulate(jnp.where(u == 0, i, 2 * u - 1), 0)
        scores(2 * u + 1, False, 0)
        accumulate(2 * u, 1)
        return carry

    lax.fori_loop(0, i // 2, body, 0)

    @pl.when(i % 2 == 1)
    def _():
        scores(i - 1, False, 1)
        accumulate(jnp.where(i == 1, i, i - 2), 0)
        accumulate(i - 1, 1)

    @pl.when(i % 2 == 0)
    def _():
        accumulate(jnp.maximum(i - 1, 0), 0)

    lane = lax.broadcasted_iota(jnp.int32, (1, LANES), 1)
    for a in range(heads // 2):
        even = acc_ref[a][:, :LANES]
        odd = acc_ref[a][:, LANES:]
        o_even = even / even[:, MLA_V_DIM:MLA_V_DIM + 1]
        o_odd = odd / odd[:, 0:1]
        o_ref[:, a * LANES:(a + 1) * LANES] = jnp.where(lane < MLA_V_DIM, o_even, o_odd).astype(BF16)


def _ffn_kernel(x_ref, o_ref, part_ref, g1_ref, wo_ref, wout_ref, gffn_ref, w1_ref, w2_ref, gfin_ref,
                out_ref, *, final):
    y_mla = jnp.dot(o_ref[...], wo_ref[...], preferred_element_type=F32)
    merged = part_ref[...] + g1_ref[...].astype(F32) * y_mla
    x1 = x_ref[...] + jnp.dot(merged.astype(BF16), wout_ref[...], preferred_element_type=F32)
    hb = _rms(x1, gffn_ref[...]).astype(BF16)
    acc = x1
    for c in range(D_FF // FF_CHUNK):
        cs = slice(c * FF_CHUNK, (c + 1) * FF_CHUNK)
        f = jnp.dot(hb, w1_ref[:, cs], preferred_element_type=F32)
        f = jnp.square(jnp.maximum(f, 0.0)).astype(BF16)
        acc = acc + jnp.dot(f, w2_ref[cs, :], preferred_element_type=F32)
    if final:
        acc = _rms(acc, gfin_ref[...])
    out_ref[...] = acc


def _layer_spec(stacked, layer):
    _, rows, cols = stacked.shape
    return pl.BlockSpec((None, rows, cols), lambda *_: (layer, 0, 0), pipeline_mode=pl.Buffered(1))


def _params(n_axes):
    return pltpu.CompilerParams(dimension_semantics=("arbitrary",) * n_axes,
                                vmem_limit_bytes=VMEM_LIMIT_BYTES)


def _mix_call(x2, lw, layer, tabs, batch, seq):
    ts = min(SEQ_TILE, seq)
    ns = seq // ts
    tok = batch * seq
    tile = lambda w: pl.BlockSpec((ts, w), lambda b, s: (b * ns + s, 0))
    tab = pl.BlockSpec((ts, LANES), lambda b, s: (s, 0))
    consts = [lw[n] for n in ("g_mix", "wsm", "wg", "bg")]
    consts2 = [lw[n] for n in ("wpg", "pscale", "wpp", "g_q", "wq", "g_kv", "wk", "wv",
                               "convw", "convb", "wax", "bax", "lam", "wlp")]
    in_specs = ([tile(D_MODEL)] + [_layer_spec(c, layer) for c in consts] + [tab, tab]
                + [_layer_spec(c, layer) for c in consts2])
    out_shape = (jax.ShapeDtypeStruct((tok, MLA_HEADS * HEAD_PAD), BF16),
                 jax.ShapeDtypeStruct((tok, MLA_HEADS * HEAD_PAD), BF16),
                 jax.ShapeDtypeStruct((tok, MLA_HEADS * HEAD_PAD), BF16),
                 jax.ShapeDtypeStruct((tok, D_MODEL), F32),
                 jax.ShapeDtypeStruct((tok, D_MODEL), BF16))
    out_specs = (tile(MLA_HEADS * HEAD_PAD), tile(MLA_HEADS * HEAD_PAD), tile(MLA_HEADS * HEAD_PAD),
                 tile(D_MODEL), tile(D_MODEL))
    return pl.pallas_call(
        functools.partial(_mix_kernel, ts=ts),
        grid=(batch, ns),
        in_specs=in_specs,
        out_specs=out_specs,
        out_shape=out_shape,
        scratch_shapes=[pltpu.VMEM((HALO, POOL_WIDTH + LRU_WIDTH), F32),
                        pltpu.VMEM((8, LRU_WIDTH), F32)],
        compiler_params=_params(2),
        name="mix",
    )(x2, *consts, *tabs, *consts2)


def _attn_call(q, k, v, batch, seq):
    tq = min(ATTN_TILE, seq)
    nq = seq // tq
    tok = batch * seq
    hg = ATTN_HEADS_PER_STEP
    width = hg * HEAD_PAD
    return pl.pallas_call(
        functools.partial(_attn_kernel, tq=tq, heads=hg),
        grid=(batch, MLA_HEADS // hg, nq),
        in_specs=[pl.BlockSpec((tq, width), lambda b, g, i: (b * nq + i, g)),
                  pl.BlockSpec((seq, width), lambda b, g, i: (b, g)),
                  pl.BlockSpec((seq, width), lambda b, g, i: (b, g))],
        out_specs=pl.BlockSpec((tq, hg * MLA_V_DIM), lambda b, g, i: (b * nq + i, g)),
        out_shape=jax.ShapeDtypeStruct((tok, MLA_WIDTH), BF16),
        scratch_shapes=[pltpu.VMEM((hg, tq, LANES), F32),
                        pltpu.VMEM((hg // 2, tq, 2 * LANES), F32),
                        pltpu.VMEM((hg // 2, tq, 2 * tq), BF16),
                        pltpu.VMEM((hg // 2, tq, 2 * LANES), F32),
                        pltpu.VMEM((hg // 2, tq, 2 * tq), BF16),
                        pltpu.VMEM((hg // 2, tq, 2 * LANES), F32)],
        compiler_params=_params(3),
        name="attn",
    )(q, k, v)


def _ffn_call(x2, o, part, g1, lw, layer, g_final, final):
    tok = x2.shape[0]
    tm = min(FFN_TILE, tok)
    tile = lambda w: pl.BlockSpec((tm, w), lambda t: (t, 0))
    consts = [lw[n] for n in ("wo", "wout", "g_ffn", "w1", "w2")]
    return pl.pallas_call(
        functools.partial(_ffn_kernel, final=final),
        grid=(tok // tm,),
        in_specs=[tile(D_MODEL), tile(MLA_WIDTH), tile(D_MODEL), tile(D_MODEL)]
                 + [_layer_spec(c, layer) for c in consts] + [_layer_spec(g_final, 0)],
        out_specs=tile(D_MODEL),
        out_shape=jax.ShapeDtypeStruct((tok, D_MODEL), F32),
        compiler_params=_params(1),
        name="ffn",
    )(x2, o, part, g1, *consts, g_final)


def _block_diag(w):
    depth, g, c, _ = w.shape
    eye = jnp.eye(g, dtype=w.dtype)
    return jnp.einsum("lgcd,gh->lgchd", w, eye).reshape(depth, g * c, g * c)


def _rope_tables(seq):
    pos = jnp.arange(seq, dtype=F32)
    inv = ROPE_THETA ** (-jnp.arange(0, MLA_ROPE_DIM, 2, dtype=F32) / MLA_ROPE_DIM)
    ang = pos[:, None] * inv[None, :]
    cos, sin = jnp.cos(ang), jnp.sin(ang)
    cos_t = _head_layout(jnp.ones((seq, MLA_NOPE_DIM), F32), jnp.concatenate([cos, cos], axis=1))
    sin_t = _head_layout(jnp.zeros((seq, MLA_NOPE_DIM), F32), jnp.concatenate([-sin, sin], axis=1))
    return cos_t, sin_t


def _head_layout(nope, rope):
    half = MLA_ROPE_DIM // 2
    split = LANES // 2 - half
    tail = jnp.zeros(nope.shape[:-1] + (HEAD_PAD - MLA_QK_DIM,), nope.dtype)
    return jnp.concatenate([rope[..., :half], nope[..., :split], rope[..., half:], nope[..., split:], tail],
                           axis=-1)


def _prep_weights(w_in, w_pool_grp, pool_scale, w_pool_proj, g_mix, g_q, w_q_up, g_kv, w_kv_up, w_mla_o,
                  conv_w, conv_b, w_lru_a, b_lru_a, w_lru_x, b_lru_x, lru_lambda, w_lru_proj, b_gate,
                  w_out, g_ffn, w_ff1, w_ff2):
    depth = w_in.shape[0]
    row = lambda a: a[:, None, :]
    w_pool = w_in[:, :, :IN_POOL_END]
    w_qlat = w_in[:, :, IN_POOL_END:IN_Q_END]
    w_ckv = w_in[:, :, IN_Q_END:IN_Q_END + MLA_KV_RANK]
    w_kr = w_in[:, :, IN_Q_END + MLA_KV_RANK:IN_KV_END]
    w_lru = w_in[:, :, IN_KV_END:IN_LRU_END]
    w_gate = w_in[:, :, IN_LRU_END:]
    kr_blk = _head_layout(jnp.zeros(w_kr.shape[:-1] + (MLA_NOPE_DIM,), w_kr.dtype), w_kr)
    wsm = jnp.concatenate([w_pool, w_lru, w_qlat, w_ckv, kr_blk], axis=2).astype(BF16)
    wq4 = w_q_up.reshape(depth, MLA_Q_RANK, MLA_HEADS, MLA_QK_DIM)
    wq = _head_layout(wq4[..., :MLA_NOPE_DIM], wq4[..., MLA_NOPE_DIM:])
    wq = wq.reshape(depth, MLA_Q_RANK, MLA_HEADS * HEAD_PAD).astype(BF16)
    wkv = w_kv_up.reshape(depth, MLA_KV_RANK, MLA_HEADS, MLA_NOPE_DIM + MLA_V_DIM)
    wk = _head_layout(wkv[..., :MLA_NOPE_DIM], jnp.zeros(wkv.shape[:-1] + (MLA_ROPE_DIM,), wkv.dtype))
    wk = wk.reshape(depth, MLA_KV_RANK, MLA_HEADS * HEAD_PAD).astype(BF16)
    wv2 = wkv[..., MLA_NOPE_DIM:].reshape(depth, MLA_KV_RANK, MLA_HEADS // 2, 2, MLA_V_DIM)
    pad_v = HEAD_PAD - MLA_V_DIM
    wv_even = jnp.pad(wv2[:, :, :, 0], ((0, 0), (0, 0), (0, 0), (0, pad_v)))
    wv_odd = jnp.pad(wv2[:, :, :, 1], ((0, 0), (0, 0), (0, 0), (pad_v, 0)))
    wv = jnp.stack([wv_even, wv_odd], axis=3).reshape(depth, MLA_KV_RANK, MLA_HEADS * HEAD_PAD).astype(BF16)
    wax = (0.5 * jnp.concatenate([_block_diag(w_lru_a), _block_diag(w_lru_x)], axis=2)).astype(BF16)
    return dict(
        g_mix=row(g_mix), wsm=wsm, wg=(0.5 * w_gate).astype(BF16), bg=row(0.5 * b_gate),
        wpg=_block_diag(w_pool_grp).astype(BF16), pscale=row(pool_scale), wpp=w_pool_proj.astype(BF16),
        g_q=row(g_q), wq=wq, g_kv=row(g_kv), wk=wk, wv=wv,
        convw=conv_w, convb=row(conv_b), wax=wax,
        bax=row(0.5 * jnp.concatenate([b_lru_a, b_lru_x], axis=1)), lam=row(lru_lambda),
        wlp=w_lru_proj.astype(BF16),
        wo=w_mla_o.astype(BF16), wout=w_out.astype(BF16), g_ffn=row(g_ffn),
        w1=w_ff1.astype(BF16), w2=w_ff2.astype(BF16))


def kernel(x, g_mix, w_in, w_pool_grp, pool_scale, w_pool_proj, g_q, w_q_up, g_kv, w_kv_up, w_mla_o, conv_w, conv_b, w_lru_a, b_lru_a, w_lru_x, b_lru_x, lru_lambda, w_lru_proj, b_gate, w_out, g_ffn, w_ff1, w_ff2, g_final):
    batch, seq, d = x.shape
    assert d == D_MODEL and seq % min(SEQ_TILE, seq) == 0 and seq >= HALO
    depth = w_in.shape[0]
    weights = _prep_weights(w_in, w_pool_grp, pool_scale, w_pool_proj, g_mix, g_q, w_q_up, g_kv, w_kv_up,
                            w_mla_o, conv_w, conv_b, w_lru_a, b_lru_a, w_lru_x, b_lru_x, lru_lambda,
                            w_lru_proj, b_gate, w_out, g_ffn, w_ff1, w_ff2)
    tabs = _rope_tables(seq)
    gfin = g_final[None, None, :]
    x2 = x.reshape(batch * seq, d)
    for l in range(depth):
        q, k, v, part, g1 = _mix_call(x2, weights, l, tabs, batch, seq)
        o = _attn_call(q, k, v, batch, seq)
        x2 = _ffn_call(x2, o, part, g1, weights, l, gfin, final=(l == depth - 1))
    return x2.reshape(batch, seq, d)
```

```python
import functools

import jax
import jax.numpy as jnp
from jax import lax
from jax.experimental import pallas as pl
from jax.experimental.pallas import tpu as pltpu

F32 = jnp.float32
BF16 = jnp.bfloat16

D_MODEL = 1024
DEPTH = 4
POOL_GROUPS = 4
POOL_GROUP_DIM = 64
POOL_WIDTH = 256
POOL_WINDOWS = (2, 4, 8, 16)
MLA_HEADS = 8
MLA_NOPE_DIM = 64
MLA_ROPE_DIM = 32
MLA_V_DIM = 64
MLA_Q_RANK = 384
MLA_KV_RANK = 256
MLA_QK_DIM = MLA_NOPE_DIM + MLA_ROPE_DIM
MLA_WIDTH = MLA_HEADS * MLA_V_DIM
ROPE_THETA = 10000.0
LRU_WIDTH = 256
CONV_WIDTH = 4
LRU_C = 8.0
D_FF = 4 * D_MODEL
EPS = 1e-6
LOG2_E = 1.4426950408889634

IN_POOL_END = POOL_WIDTH
IN_Q_END = IN_POOL_END + MLA_Q_RANK
IN_KV_END = IN_Q_END + MLA_KV_RANK + MLA_ROPE_DIM
IN_LRU_END = IN_KV_END + LRU_WIDTH

LANES = 128
SUBLANES = 8
HEAD_PAD = LANES
HALO = 16
SMALL_COLS = POOL_WIDTH + LRU_WIDTH + MLA_Q_RANK + MLA_KV_RANK + LANES
VMEM_LIMIT_BYTES = 56 * 1024 * 1024
MASK_VALUE = -1e30

SEQ_TILE = 512
ATTN_TILE = 512
ATTN_HEADS_PER_STEP = 4
FFN_TILE = 512
FF_CHUNK = 1024


def _rms(x, g):
    return x * lax.rsqrt(jnp.mean(x * x, axis=-1, keepdims=True) + EPS) * g


def _sigmoid_of_twice(half_x):
    return 0.5 * jnp.tanh(half_x) + 0.5


def _rope(blk, cos, sin):
    return blk * cos + pltpu.roll(blk, LANES // 2, 1) * sin


def _mix_kernel(x_ref, gmix_ref, wsm_ref, wg_ref, bg_ref, cos_ref, sin_ref,
                wpg_ref, pscale_ref, wpp_ref, gq_ref, wq_ref, gkv_ref, wk_ref, wv_ref,
                convw_ref, convb_ref, wax_ref, bax_ref, lam_ref, wlp_ref,
                q_ref, k_ref, v_ref, part_ref, g1_ref,
                halo_ref, h_ref, *, ts):
    s = pl.program_id(1)

    @pl.when(s == 0)
    def _():
        halo_ref[...] = jnp.zeros_like(halo_ref)
        h_ref[...] = jnp.zeros_like(h_ref)

    hb = _rms(x_ref[...], gmix_ref[...]).astype(BF16)
    small = jnp.dot(hb, wsm_ref[...], preferred_element_type=F32)

    def gate_logit(j):
        return jnp.dot(hb, wg_ref[:, j * D_MODEL:(j + 1) * D_MODEL], preferred_element_type=F32)

    def gate(j, z):
        return _sigmoid_of_twice(z + bg_ref[:, j * D_MODEL:(j + 1) * D_MODEL])

    u2 = small[:, :POOL_WIDTH + LRU_WIDTH]
    ext = jnp.concatenate([halo_ref[...], u2], axis=0)
    halo_ref[...] = u2[ts - HALO:, :]
    ep = ext[:, :POOL_WIDTH]
    el = ext[:, POOL_WIDTH:]
    row = lax.broadcasted_iota(jnp.int32, (ts, 1), 0)

    cw = convw_ref[...]
    uc = (cw[3:4, :] * el + cw[2:3, :] * pltpu.roll(el, 1, 0)
          + cw[1:2, :] * pltpu.roll(el, 2, 0) + cw[0:1, :] * pltpu.roll(el, 3, 0))
    uc = uc[HALO:] + convb_ref[...]
    ri = _sigmoid_of_twice(jnp.dot(uc.astype(BF16), wax_ref[...], preferred_element_type=F32) + bax_ref[...])
    z_lru = gate_logit(2)
    g1_ref[...] = gate(1, gate_logit(1)).astype(BF16)
    z_pool = gate_logit(0)
    r = ri[:, :LRU_WIDTH]
    ig = ri[:, LRU_WIDTH:]
    nlam = -lam_ref[...]
    softplus = jnp.maximum(nlam, 0.0) + jnp.log1p(jnp.exp(-jnp.abs(nlam)))
    log_a = (-LRU_C) * r * softplus
    a_cum = jnp.exp(log_a)
    b_cum = jnp.sqrt(-jnp.tanh(log_a) * (a_cum * a_cum + 1.0)) * (ig * uc)
    sub = lax.broadcasted_iota(jnp.int32, (1, SUBLANES, 1), 1)
    a3 = a_cum.reshape(ts // SUBLANES, SUBLANES, LRU_WIDTH)
    b3 = b_cum.reshape(ts // SUBLANES, SUBLANES, LRU_WIDTH)
    k = 1
    while k < SUBLANES:
        keep = sub >= k
        a_sh = jnp.where(keep, pltpu.roll(a3, k, 1), 1.0)
        b_sh = jnp.where(keep, pltpu.roll(b3, k, 1), 0.0)
        b3 = a3 * b_sh + b3
        a3 = a3 * a_sh
        k *= 2
    carry = h_ref[...]
    tiles = []
    for t in range(ts // SUBLANES):
        h_t = a3[t] * carry + b3[t]
        tiles.append(h_t)
        carry = jnp.broadcast_to(h_t[SUBLANES - 1:, :], h_t.shape)
    hh = jnp.concatenate(tiles, axis=0)
    h_ref[...] = carry
    y_lru = jnp.dot(hh.astype(BF16), wlp_ref[...], preferred_element_type=F32)

    part_lru = gate(2, z_lru) * y_lru

    s2 = ep + pltpu.roll(ep, 1, 0)
    s4 = s2 + pltpu.roll(s2, 2, 0)
    s8 = s4 + pltpu.roll(s4, 4, 0)
    s16 = s8 + pltpu.roll(s8, 8, 0)
    lane = lax.broadcasted_iota(jnp.int32, (1, POOL_WIDTH), 1)
    grp = lane // POOL_GROUP_DIM
    wsum = jnp.where(grp == 0, s2, jnp.where(grp == 1, s4, jnp.where(grp == 2, s8, s16)))[HALO:]
    win = jnp.where(grp == 0, float(POOL_WINDOWS[0]),
                    jnp.where(grp == 1, float(POOL_WINDOWS[1]),
                              jnp.where(grp == 2, float(POOL_WINDOWS[2]), float(POOL_WINDOWS[3]))))
    tpos = (s * ts + row + 1).astype(F32)
    cnt = jnp.minimum(tpos, win)
    mixed = wsum / cnt - u2[:, :POOL_WIDTH]
    yp = jnp.dot(mixed.astype(BF16), wpg_ref[...], preferred_element_type=F32) * pscale_ref[...]
    y_pool = jnp.dot(yp.astype(BF16), wpp_ref[...], preferred_element_type=F32)
    part_ref[...] = gate(0, z_pool) * y_pool + part_lru

    cos = cos_ref[...]
    sin = sin_ref[...]
    q0 = POOL_WIDTH + LRU_WIDTH
    c0 = q0 + MLA_Q_RANK
    r0 = c0 + MLA_KV_RANK
    qn = _rms(small[:, q0:c0], gq_ref[...]) * (MLA_QK_DIM ** -0.5 * LOG2_E)
    qf = jnp.dot(qn.astype(BF16), wq_ref[...], preferred_element_type=F32)
    cn = _rms(small[:, c0:r0], gkv_ref[...]).astype(BF16)
    kf = jnp.dot(cn, wk_ref[...], preferred_element_type=F32)
    kr = _rope(small[:, r0:r0 + LANES], cos, sin)
    for hd in range(MLA_HEADS):
        sl = slice(hd * HEAD_PAD, (hd + 1) * HEAD_PAD)
        q_ref[:, sl] = _rope(qf[:, sl], cos, sin).astype(BF16)
        k_ref[:, sl] = (kf[:, sl] + kr).astype(BF16)
    vlane = lax.broadcasted_iota(jnp.int32, (1, MLA_HEADS * HEAD_PAD), 1)
    vpos = vlane % (2 * HEAD_PAD)
    vone = jnp.where((vpos == MLA_V_DIM) | (vpos == HEAD_PAD), 1.0, 0.0)
    v_ref[...] = (jnp.dot(cn, wv_ref[...], preferred_element_type=F32) + vone).astype(BF16)


def _attn_kernel(q_ref, k_ref, v_ref, o_ref, m_ref, alpha0_ref, p0_ref, alpha1_ref, p1_ref, acc_ref,
                 *, tq, heads):
    i = pl.program_id(2)
    bufs = ((p0_ref, alpha0_ref), (p1_ref, alpha1_ref))
    row = lax.broadcasted_iota(jnp.int32, (tq, tq), 0)
    col = lax.broadcasted_iota(jnp.int32, (tq, tq), 1)
    causal = col <= row
    for j in range(heads):
        m_ref[j] = jnp.full((tq, LANES), MASK_VALUE, F32)
    for a in range(heads // 2):
        acc_ref[a] = jnp.zeros((tq, 2 * LANES), F32)
    zero_v = jnp.zeros((tq, HEAD_PAD), BF16)

    all_pairs = tuple(range(heads // 2))

    def scores(c, masked, buf, pairs=all_pairs):
        p_ref, alpha_ref = bufs[buf]
        start = pl.multiple_of(c * tq, tq)
        for j in [2 * a + half for a in pairs for half in (0, 1)]:
            hs = slice(j * HEAD_PAD, (j + 1) * HEAD_PAD)
            a, half = divmod(j, 2)
            kc = k_ref[pl.ds(start, tq), hs]
            sc = lax.dot_general(q_ref[:, hs], kc, (((1,), (1,)), ((), ())), preferred_element_type=F32)
            if masked:
                sc = jnp.where(causal, sc, MASK_VALUE)
            m_old = m_ref[j]
            m_new = jnp.maximum(m_old, jnp.max(sc, axis=-1, keepdims=True))
            p = jnp.exp2(sc - jnp.concatenate([m_new] * (tq // LANES), axis=1))
            p_ref[a, :, half * tq:(half + 1) * tq] = p.astype(BF16)
            alpha_ref[a, :, half * LANES:(half + 1) * LANES] = jnp.exp2(m_old - m_new)
            m_ref[j] = m_new

    def accumulate(c, buf, pairs=all_pairs):
        p_ref, alpha_ref = bufs[buf]
        start = pl.multiple_of(c * tq, tq)
        for a in pairs:
            v_even = v_ref[pl.ds(start, tq), (2 * a) * HEAD_PAD:(2 * a + 1) * HEAD_PAD]
            v_odd = v_ref[pl.ds(start, tq), (2 * a + 1) * HEAD_PAD:(2 * a + 2) * HEAD_PAD]
            v2 = jnp.concatenate([jnp.concatenate([v_even, zero_v], axis=1),
                                  jnp.concatenate([zero_v, v_odd], axis=1)], axis=0)
            acc_ref[a] = alpha_ref[a] * acc_ref[a] + jnp.dot(p_ref[a], v2, preferred_element_type=F32)

    scores(i, True, 0)

    def step(c_new, buf_new, c_old, buf_old):
        for a in all_pairs:
            scores(c_new, False, buf_new, (a,))
            accumulate(c_old, buf_old, (a,))

    def body(u, carry):
        step(2 * u, 1, jnp.where(u == 0, i, 2 * u - 1), 0)
        step(2 * u + 1, 0, 2 * u, 1)
        return carry

    lax.fori_loop(0, i // 2, body, 0)

    @pl.when(i % 2 == 1)
    def _():
        step(i - 1, 1, jnp.where(i == 1, i, i - 2), 0)
        accumulate(i - 1, 1)

    @pl.when(i % 2 == 0)
    def _():
        accumulate(jnp.maximum(i - 1, 0), 0)

    lane = lax.broadcasted_iota(jnp.int32, (1, LANES), 1)
    for a in range(heads // 2):
        even = acc_ref[a][:, :LANES]
        odd = acc_ref[a][:, LANES:]
        o_even = even / even[:, MLA_V_DIM:MLA_V_DIM + 1]
        o_odd = odd / odd[:, 0:1]
        o_ref[:, a * LANES:(a + 1) * LANES] = jnp.where(lane < MLA_V_DIM, o_even, o_odd).astype(BF16)


def _ffn_kernel(x_ref, o_ref, part_ref, g1_ref, wo_ref, wout_ref, gffn_ref, w1_ref, w2_ref, gfin_ref,
                out_ref, *, final):
    y_mla = jnp.dot(o_ref[...], wo_ref[...], preferred_element_type=F32)
    merged = part_ref[...] + g1_ref[...].astype(F32) * y_mla
    x1 = x_ref[...] + jnp.dot(merged.astype(BF16), wout_ref[...], preferred_element_type=F32)
    hb = _rms(x1, gffn_ref[...]).astype(BF16)
    acc = x1
    for c in range(D_FF // FF_CHUNK):
        cs = slice(c * FF_CHUNK, (c + 1) * FF_CHUNK)
        f = jnp.dot(hb, w1_ref[:, cs], preferred_element_type=F32)
        f = jnp.square(jnp.maximum(f, 0.0)).astype(BF16)
        acc = acc + jnp.dot(f, w2_ref[cs, :], preferred_element_type=F32)
    if final:
        acc = _rms(acc, gfin_ref[...])
    out_ref[...] = acc


def _layer_spec(stacked, layer):
    _, rows, cols = stacked.shape
    return pl.BlockSpec((None, rows, cols), lambda *_: (layer, 0, 0), pipeline_mode=pl.Buffered(1))


def _params(n_axes):
    return pltpu.CompilerParams(dimension_semantics=("arbitrary",) * n_axes,
                                vmem_limit_bytes=VMEM_LIMIT_BYTES)


def _mix_call(x2, lw, layer, tabs, batch, seq):
    ts = min(SEQ_TILE, seq)
    ns = seq // ts
    tok = batch * seq
    tile = lambda w: pl.BlockSpec((ts, w), lambda b, s: (b * ns + s, 0))
    tab = pl.BlockSpec((ts, LANES), lambda b, s: (s, 0))
    consts = [lw[n] for n in ("g_mix", "wsm", "wg", "bg")]
    consts2 = [lw[n] for n in ("wpg", "pscale", "wpp", "g_q", "wq", "g_kv", "wk", "wv",
                               "convw", "convb", "wax", "bax", "lam", "wlp")]
    in_specs = ([tile(D_MODEL)] + [_layer_spec(c, layer) for c in consts] + [tab, tab]
                + [_layer_spec(c, layer) for c in consts2])
    out_shape = (jax.ShapeDtypeStruct((tok, MLA_HEADS * HEAD_PAD), BF16),
                 jax.ShapeDtypeStruct((tok, MLA_HEADS * HEAD_PAD), BF16),
                 jax.ShapeDtypeStruct((tok, MLA_HEADS * HEAD_PAD), BF16),
                 jax.ShapeDtypeStruct((tok, D_MODEL), F32),
                 jax.ShapeDtypeStruct((tok, D_MODEL), BF16))
    out_specs = (tile(MLA_HEADS * HEAD_PAD), tile(MLA_HEADS * HEAD_PAD), tile(MLA_HEADS * HEAD_PAD),
                 tile(D_MODEL), tile(D_MODEL))
    return pl.pallas_call(
        functools.partial(_mix_kernel, ts=ts),
        grid=(batch, ns),
        in_specs=in_specs,
        out_specs=out_specs,
        out_shape=out_shape,
        scratch_shapes=[pltpu.VMEM((HALO, POOL_WIDTH + LRU_WIDTH), F32),
                        pltpu.VMEM((8, LRU_WIDTH), F32)],
        compiler_params=_params(2),
        name="mix",
    )(x2, *consts, *tabs, *consts2)


def _attn_call(q, k, v, batch, seq):
    tq = min(ATTN_TILE, seq)
    nq = seq // tq
    tok = batch * seq
    hg = ATTN_HEADS_PER_STEP
    width = hg * HEAD_PAD
    return pl.pallas_call(
        functools.partial(_attn_kernel, tq=tq, heads=hg),
        grid=(batch, MLA_HEADS // hg, nq),
        in_specs=[pl.BlockSpec((tq, width), lambda b, g, i: (b * nq + i, g)),
                  pl.BlockSpec((seq, width), lambda b, g, i: (b, g)),
                  pl.BlockSpec((seq, width), lambda b, g, i: (b, g))],
        out_specs=pl.BlockSpec((tq, hg * MLA_V_DIM), lambda b, g, i: (b * nq + i, g)),
        out_shape=jax.ShapeDtypeStruct((tok, MLA_WIDTH), BF16),
        scratch_shapes=[pltpu.VMEM((hg, tq, LANES), F32),
                        pltpu.VMEM((hg // 2, tq, 2 * LANES), F32),
                        pltpu.VMEM((hg // 2, tq, 2 * tq), BF16),
                        pltpu.VMEM((hg // 2, tq, 2 * LANES), F32),
                        pltpu.VMEM((hg // 2, tq, 2 * tq), BF16),
                        pltpu.VMEM((hg // 2, tq, 2 * LANES), F32)],
        compiler_params=_params(3),
        name="attn",
    )(q, k, v)


def _ffn_call(x2, o, part, g1, lw, layer, g_final, final):
    tok = x2.shape[0]
    tm = min(FFN_TILE, tok)
    tile = lambda w: pl.BlockSpec((tm, w), lambda t: (t, 0))
    consts = [lw[n] for n in ("wo", "wout", "g_ffn", "w1", "w2")]
    return pl.pallas_call(
        functools.partial(_ffn_kernel, final=final),
        grid=(tok // tm,),
        in_specs=[tile(D_MODEL), tile(MLA_WIDTH), tile(D_MODEL), tile(D_MODEL)]
                 + [_layer_spec(c, layer) for c in consts] + [_layer_spec(g_final, 0)],
        out_specs=tile(D_MODEL),
        out_shape=jax.ShapeDtypeStruct((tok, D_MODEL), F32),
        compiler_params=_params(1),
        name="ffn",
    )(x2, o, part, g1, *consts, g_final)


def _block_diag(w):
    depth, g, c, _ = w.shape
    eye = jnp.eye(g, dtype=w.dtype)
    return jnp.einsum("lgcd,gh->lgchd", w, eye).reshape(depth, g * c, g * c)


def _rope_tables(seq):
    pos = jnp.arange(seq, dtype=F32)
    inv = ROPE_THETA ** (-jnp.arange(0, MLA_ROPE_DIM, 2, dtype=F32) / MLA_ROPE_DIM)
    ang = pos[:, None] * inv[None, :]
    cos, sin = jnp.cos(ang), jnp.sin(ang)
    cos_t = _head_layout(jnp.ones((seq, MLA_NOPE_DIM), F32), jnp.concatenate([cos, cos], axis=1))
    sin_t = _head_layout(jnp.zeros((seq, MLA_NOPE_DIM), F32), jnp.concatenate([-sin, sin], axis=1))
    return cos_t, sin_t


def _head_layout(nope, rope):
    half = MLA_ROPE_DIM // 2
    split = LANES // 2 - half
    tail = jnp.zeros(nope.shape[:-1] + (HEAD_PAD - MLA_QK_DIM,), nope.dtype)
    return jnp.concatenate([rope[..., :half], nope[..., :split], rope[..., half:], nope[..., split:], tail],
                           axis=-1)


def _prep_weights(w_in, w_pool_grp, pool_scale, w_pool_proj, g_mix, g_q, w_q_up, g_kv, w_kv_up, w_mla_o,
                  conv_w, conv_b, w_lru_a, b_lru_a, w_lru_x, b_lru_x, lru_lambda, w_lru_proj, b_gate,
                  w_out, g_ffn, w_ff1, w_ff2):
    depth = w_in.shape[0]
    row = lambda a: a[:, None, :]
    w_pool = w_in[:, :, :IN_POOL_END]
    w_qlat = w_in[:, :, IN_POOL_END:IN_Q_END]
    w_ckv = w_in[:, :, IN_Q_END:IN_Q_END + MLA_KV_RANK]
    w_kr = w_in[:, :, IN_Q_END + MLA_KV_RANK:IN_KV_END]
    w_lru = w_in[:, :, IN_KV_END:IN_LRU_END]
    w_gate = w_in[:, :, IN_LRU_END:]
    kr_blk = _head_layout(jnp.zeros(w_kr.shape[:-1] + (MLA_NOPE_DIM,), w_kr.dtype), w_kr)
    wsm = jnp.concatenate([w_pool, w_lru, w_qlat, w_ckv, kr_blk], axis=2).astype(BF16)
    wq4 = w_q_up.reshape(depth, MLA_Q_RANK, MLA_HEADS, MLA_QK_DIM)
    wq = _head_layout(wq4[..., :MLA_NOPE_DIM], wq4[..., MLA_NOPE_DIM:])
    wq = wq.reshape(depth, MLA_Q_RANK, MLA_HEADS * HEAD_PAD).astype(BF16)
    wkv = w_kv_up.reshape(depth, MLA_KV_RANK, MLA_HEADS, MLA_NOPE_DIM + MLA_V_DIM)
    wk = _head_layout(wkv[..., :MLA_NOPE_DIM], jnp.zeros(wkv.shape[:-1] + (MLA_ROPE_DIM,), wkv.dtype))
    wk = wk.reshape(depth, MLA_KV_RANK, MLA_HEADS * HEAD_PAD).astype(BF16)
    wv2 = wkv[..., MLA_NOPE_DIM:].reshape(depth, MLA_KV_RANK, MLA_HEADS // 2, 2, MLA_V_DIM)
    pad_v = HEAD_PAD - MLA_V_DIM
    wv_even = jnp.pad(wv2[:, :, :, 0], ((0, 0), (0, 0), (0, 0), (0, pad_v)))
    wv_odd = jnp.pad(wv2[:, :, :, 1], ((0, 0), (0, 0), (0, 0), (pad_v, 0)))
    wv = jnp.stack([wv_even, wv_odd], axis=3).reshape(depth, MLA_KV_RANK, MLA_HEADS * HEAD_PAD).astype(BF16)
    wax = (0.5 * jnp.concatenate([_block_diag(w_lru_a), _block_diag(w_lru_x)], axis=2)).astype(BF16)
    return dict(
        g_mix=row(g_mix), wsm=wsm, wg=(0.5 * w_gate).astype(BF16), bg=row(0.5 * b_gate),
        wpg=_block_diag(w_pool_grp).astype(BF16), pscale=row(pool_scale), wpp=w_pool_proj.astype(BF16),
        g_q=row(g_q), wq=wq, g_kv=row(g_kv), wk=wk, wv=wv,
        convw=conv_w, convb=row(conv_b), wax=wax,
        bax=row(0.5 * jnp.concatenate([b_lru_a, b_lru_x], axis=1)), lam=row(lru_lambda),
        wlp=w_lru_proj.astype(BF16),
        wo=w_mla_o.astype(BF16), wout=w_out.astype(BF16), g_ffn=row(g_ffn),
        w1=w_ff1.astype(BF16), w2=w_ff2.astype(BF16))


def kernel(x, g_mix, w_in, w_pool_grp, pool_scale, w_pool_proj, g_q, w_q_up, g_kv, w_kv_up, w_mla_o, conv_w, conv_b, w_lru_a, b_lru_a, w_lru_x, b_lru_x, lru_lambda, w_lru_proj, b_gate, w_out, g_ffn, w_ff1, w_ff2, g_final):
    batch, seq, d = x.shape
    assert d == D_MODEL and seq % min(SEQ_TILE, seq) == 0 and seq >= HALO
    depth = w_in.shape[0]
    weights = _prep_weights(w_in, w_pool_grp, pool_scale, w_pool_proj, g_mix, g_q, w_q_up, g_kv, w_kv_up,
                            w_mla_o, conv_w, conv_b, w_lru_a, b_lru_a, w_lru_x, b_lru_x, lru_lambda,
                            w_lru_proj, b_gate, w_out, g_ffn, w_ff1, w_ff2)
    tabs = _rope_tables(seq)
    gfin = g_final[None, None, :]
    x2 = x.reshape(batch * seq, d)
    for l in range(depth):
        q, k, v, part, g1 = _mix_call(x2, weights, l, tabs, batch, seq)
        o = _attn_call(q, k, v, batch, seq)
        x2 = _ffn_call(x2, o, part, g1, weights, l, gfin, final=(l == depth - 1))
    return x2.reshape(batch, seq, d)
```

```python
import functools

import jax
import jax.numpy as jnp
from jax import lax
from jax.experimental import pallas as pl
from jax.experimental.pallas import tpu as pltpu

F32 = jnp.float32
BF16 = jnp.bfloat16

D_MODEL = 1024
DEPTH = 4
POOL_GROUPS = 4
POOL_GROUP_DIM = 64
POOL_WIDTH = 256
POOL_WINDOWS = (2, 4, 8, 16)
MLA_HEADS = 8
MLA_NOPE_DIM = 64
MLA_ROPE_DIM = 32
MLA_V_DIM = 64
MLA_Q_RANK = 384
MLA_KV_RANK = 256
MLA_QK_DIM = MLA_NOPE_DIM + MLA_ROPE_DIM
MLA_WIDTH = MLA_HEADS * MLA_V_DIM
ROPE_THETA = 10000.0
LRU_WIDTH = 256
CONV_WIDTH = 4
LRU_C = 8.0
D_FF = 4 * D_MODEL
EPS = 1e-6
LOG2_E = 1.4426950408889634

IN_POOL_END = POOL_WIDTH
IN_Q_END = IN_POOL_END + MLA_Q_RANK
IN_KV_END = IN_Q_END + MLA_KV_RANK + MLA_ROPE_DIM
IN_LRU_END = IN_KV_END + LRU_WIDTH

LANES = 128
SUBLANES = 8
HEAD_PAD = LANES
HALO = 16
SMALL_COLS = POOL_WIDTH + LRU_WIDTH + MLA_Q_RANK + MLA_KV_RANK + LANES
VMEM_LIMIT_BYTES = 56 * 1024 * 1024
MASK_VALUE = -1e30

SEQ_TILE = 512
ATTN_TILE = 512
ATTN_HEADS_PER_STEP = 4
FFN_TILE = 512
FF_CHUNK = 1024


def _rms(x, g):
    return x * lax.rsqrt(jnp.mean(x * x, axis=-1, keepdims=True) + EPS) * g


def _sigmoid_of_twice(half_x):
    return 0.5 * jnp.tanh(half_x) + 0.5


def _rope(blk, cos, sin):
    return blk * cos + pltpu.roll(blk, LANES // 2, 1) * sin


def _mix_kernel(x_ref, gmix_ref, wsm_ref, wg_ref, bg_ref, cos_ref, sin_ref,
                wpg_ref, pscale_ref, wpp_ref, gq_ref, wq_ref, gkv_ref, wk_ref, wv_ref,
                convw_ref, convb_ref, wax_ref, bax_ref, lam_ref, wlp_ref,
                q_ref, k_ref, v_ref, part_ref, g1_ref,
                halo_ref, h_ref, *, ts):
    s = pl.program_id(1)

    @pl.when(s == 0)
    def _():
        halo_ref[...] = jnp.zeros_like(halo_ref)
        h_ref[...] = jnp.zeros_like(h_ref)

    hb = _rms(x_ref[...], gmix_ref[...]).astype(BF16)
    small = jnp.dot(hb, wsm_ref[...], preferred_element_type=F32)

    def gate_logit(j):
        return jnp.dot(hb, wg_ref[:, j * D_MODEL:(j + 1) * D_MODEL], preferred_element_type=F32)

    def gate(j, z):
        return _sigmoid_of_twice(z + bg_ref[:, j * D_MODEL:(j + 1) * D_MODEL])

    u2 = small[:, :POOL_WIDTH + LRU_WIDTH]
    ext = jnp.concatenate([halo_ref[...], u2], axis=0)
    halo_ref[...] = u2[ts - HALO:, :]
    ep = ext[:, :POOL_WIDTH]
    el = ext[:, POOL_WIDTH:]
    row = lax.broadcasted_iota(jnp.int32, (ts, 1), 0)

    cw = convw_ref[...]
    uc = (cw[3:4, :] * el + cw[2:3, :] * pltpu.roll(el, 1, 0)
          + cw[1:2, :] * pltpu.roll(el, 2, 0) + cw[0:1, :] * pltpu.roll(el, 3, 0))
    uc = uc[HALO:] + convb_ref[...]
    ri = _sigmoid_of_twice(jnp.dot(uc.astype(BF16), wax_ref[...], preferred_element_type=F32) + bax_ref[...])
    z_lru = gate_logit(2)
    g1_ref[...] = gate(1, gate_logit(1)).astype(BF16)
    z_pool = gate_logit(0)
    r = ri[:, :LRU_WIDTH]
    ig = ri[:, LRU_WIDTH:]
    nlam = -lam_ref[...]
    softplus = jnp.maximum(nlam, 0.0) + jnp.log1p(jnp.exp(-jnp.abs(nlam)))
    log_a = (-LRU_C) * r * softplus
    a_cum = jnp.exp(log_a)
    b_cum = jnp.sqrt(-jnp.tanh(log_a) * (a_cum * a_cum + 1.0)) * (ig * uc)
    sub = lax.broadcasted_iota(jnp.int32, (1, SUBLANES, 1), 1)
    a3 = a_cum.reshape(ts // SUBLANES, SUBLANES, LRU_WIDTH)
    b3 = b_cum.reshape(ts // SUBLANES, SUBLANES, LRU_WIDTH)
    k = 1
    while k < SUBLANES:
        keep = sub >= k
        a_sh = jnp.where(keep, pltpu.roll(a3, k, 1), 1.0)
        b_sh = jnp.where(keep, pltpu.roll(b3, k, 1), 0.0)
        b3 = a3 * b_sh + b3
        a3 = a3 * a_sh
        k *= 2
    carry = h_ref[...]
    tiles = []
    for t in range(ts // SUBLANES):
        h_t = a3[t] * carry + b3[t]
        tiles.append(h_t)
        carry = jnp.broadcast_to(h_t[SUBLANES - 1:, :], h_t.shape)
    hh = jnp.concatenate(tiles, axis=0)
    h_ref[...] = carry
    y_lru = jnp.dot(hh.astype(BF16), wlp_ref[...], preferred_element_type=F32)

    part_lru = gate(2, z_lru) * y_lru

    s2 = ep + pltpu.roll(ep, 1, 0)
    s4 = s2 + pltpu.roll(s2, 2, 0)
    s8 = s4 + pltpu.roll(s4, 4, 0)
    s16 = s8 + pltpu.roll(s8, 8, 0)
    lane = lax.broadcasted_iota(jnp.int32, (1, POOL_WIDTH), 1)
    grp = lane // POOL_GROUP_DIM
    wsum = jnp.where(grp == 0, s2, jnp.where(grp == 1, s4, jnp.where(grp == 2, s8, s16)))[HALO:]
    win = jnp.where(grp == 0, float(POOL_WINDOWS[0]),
                    jnp.where(grp == 1, float(POOL_WINDOWS[1]),
                              jnp.where(grp == 2, float(POOL_WINDOWS[2]), float(POOL_WINDOWS[3]))))
    tpos = (s * ts + row + 1).astype(F32)
    cnt = jnp.minimum(tpos, win)
    mixed = wsum / cnt - u2[:, :POOL_WIDTH]
    yp = jnp.dot(mixed.astype(BF16), wpg_ref[...], preferred_element_type=F32) * pscale_ref[...]
    y_pool = jnp.dot(yp.astype(BF16), wpp_ref[...], preferred_element_type=F32)
    part_ref[...] = gate(0, z_pool) * y_pool + part_lru

    cos = cos_ref[...]
    sin = sin_ref[...]
    q0 = POOL_WIDTH + LRU_WIDTH
    c0 = q0 + MLA_Q_RANK
    r0 = c0 + MLA_KV_RANK
    qn = _rms(small[:, q0:c0], gq_ref[...]) * (MLA_QK_DIM ** -0.5 * LOG2_E)
    qf = jnp.dot(qn.astype(BF16), wq_ref[...], preferred_element_type=F32)
    cn = _rms(small[:, c0:r0], gkv_ref[...]).astype(BF16)
    kf = jnp.dot(cn, wk_ref[...], preferred_element_type=F32)
    kr = _rope(small[:, r0:r0 + LANES], cos, sin)
    for hd in range(MLA_HEADS):
        sl = slice(hd * HEAD_PAD, (hd + 1) * HEAD_PAD)
        q_ref[:, sl] = _rope(qf[:, sl], cos, sin).astype(BF16)
        k_ref[:, sl] = (kf[:, sl] + kr).astype(BF16)
    vlane = lax.broadcasted_iota(jnp.int32, (1, MLA_HEADS * HEAD_PAD), 1)
    vpos = vlane % (2 * HEAD_PAD)
    vone = jnp.where((vpos == MLA_V_DIM) | (vpos == HEAD_PAD), 1.0, 0.0)
    v_ref[...] = (jnp.dot(cn, wv_ref[...], preferred_element_type=F32) + vone).astype(BF16)


def _attn_kernel(qa_ref, qb_ref, k_ref, v_ref, o_in_ref, o_ref, m_ref, alpha0_ref, p0_ref, alpha1_ref, p1_ref,
                 acc_ref, *, tq, heads, tiles):
    del o_in_ref
    q_refs = (qa_ref, qb_ref)
    bufs = ((p0_ref, alpha0_ref), (p1_ref, alpha1_ref))
    all_pairs = tuple(range(heads // 2))
    row = lax.broadcasted_iota(jnp.int32, (tq, tq), 0)
    col = lax.broadcasted_iota(jnp.int32, (tq, tq), 1)
    causal = col <= row
    zero_v = jnp.zeros((tq, HEAD_PAD), BF16)
    for t in range(2):
        for j in range(heads):
            m_ref[t, j] = jnp.full((tq, LANES), MASK_VALUE, F32)
        for a in all_pairs:
            acc_ref[t, a] = jnp.zeros((tq, 2 * LANES), F32)

    steps = []
    for t, tile in enumerate(tiles):
        steps.append((t, tile, True))
        steps.extend((t, c, False) for c in range(tile))

    def scores(n, pairs):
        t, c, masked = steps[n]
        p_ref, alpha_ref = bufs[n % 2]
        for j in [2 * a + half for a in pairs for half in (0, 1)]:
            hs = slice(j * HEAD_PAD, (j + 1) * HEAD_PAD)
            a, half = divmod(j, 2)
            kc = k_ref[c * tq:(c + 1) * tq, hs]
            sc = lax.dot_general(q_refs[t][:, hs], kc, (((1,), (1,)), ((), ())), preferred_element_type=F32)
            if masked:
                sc = jnp.where(causal, sc, MASK_VALUE)
            m_old = m_ref[t, j]
            m_new = jnp.maximum(m_old, jnp.max(sc, axis=-1, keepdims=True))
            p = jnp.exp2(sc - jnp.concatenate([m_new] * (tq // LANES), axis=1))
            p_ref[a, :, half * tq:(half + 1) * tq] = p.astype(BF16)
            alpha_ref[a, :, half * LANES:(half + 1) * LANES] = jnp.exp2(m_old - m_new)
            m_ref[t, j] = m_new

    def accumulate(n, pairs):
        t, c, _ = steps[n]
        p_ref, alpha_ref = bufs[n % 2]
        for a in pairs:
            v_even = v_ref[c * tq:(c + 1) * tq, (2 * a) * HEAD_PAD:(2 * a + 1) * HEAD_PAD]
            v_odd = v_ref[c * tq:(c + 1) * tq, (2 * a + 1) * HEAD_PAD:(2 * a + 2) * HEAD_PAD]
            v2 = jnp.concatenate([jnp.concatenate([v_even, zero_v], axis=1),
                                  jnp.concatenate([zero_v, v_odd], axis=1)], axis=0)
            acc_ref[t, a] = alpha_ref[a] * acc_ref[t, a] + jnp.dot(p_ref[a], v2, preferred_element_type=F32)

    scores(0, all_pairs)
    for n in range(1, len(steps)):
        for a in all_pairs:
            scores(n, (a,))
            accumulate(n - 1, (a,))
    accumulate(len(steps) - 1, all_pairs)

    lane = lax.broadcasted_iota(jnp.int32, (1, LANES), 1)
    for t in range(2):
        for a in all_pairs:
            even = acc_ref[t, a][:, :LANES]
            odd = acc_ref[t, a][:, LANES:]
            o_even = even / even[:, MLA_V_DIM:MLA_V_DIM + 1]
            o_odd = odd / odd[:, 0:1]
            o_ref[t, :, a * LANES:(a + 1) * LANES] = jnp.where(lane < MLA_V_DIM, o_even, o_odd).astype(BF16)


def _ffn_kernel(x_ref, o_ref, part_ref, g1_ref, wo_ref, wout_ref, gffn_ref, w1_ref, w2_ref, gfin_ref,
                out_ref, *, final):
    y_mla = jnp.dot(o_ref[...], wo_ref[...], preferred_element_type=F32)
    merged = part_ref[...] + g1_ref[...].astype(F32) * y_mla
    x1 = x_ref[...] + jnp.dot(merged.astype(BF16), wout_ref[...], preferred_element_type=F32)
    hb = _rms(x1, gffn_ref[...]).astype(BF16)
    acc = x1
    for c in range(D_FF // FF_CHUNK):
        cs = slice(c * FF_CHUNK, (c + 1) * FF_CHUNK)
        f = jnp.dot(hb, w1_ref[:, cs], preferred_element_type=F32)
        f = jnp.square(jnp.maximum(f, 0.0)).astype(BF16)
        acc = acc + jnp.dot(f, w2_ref[cs, :], preferred_element_type=F32)
    if final:
        acc = _rms(acc, gfin_ref[...])
    out_ref[...] = acc


def _layer_spec(stacked, layer):
    _, rows, cols = stacked.shape
    return pl.BlockSpec((None, rows, cols), lambda *_: (layer, 0, 0), pipeline_mode=pl.Buffered(1))


def _params(n_axes):
    return pltpu.CompilerParams(dimension_semantics=("arbitrary",) * n_axes,
                                vmem_limit_bytes=VMEM_LIMIT_BYTES)


def _mix_call(x2, lw, layer, tabs, batch, seq):
    ts = min(SEQ_TILE, seq)
    ns = seq // ts
    tok = batch * seq
    tile = lambda w: pl.BlockSpec((ts, w), lambda b, s: (b * ns + s, 0))
    tab = pl.BlockSpec((ts, LANES), lambda b, s: (s, 0))
    consts = [lw[n] for n in ("g_mix", "wsm", "wg", "bg")]
    consts2 = [lw[n] for n in ("wpg", "pscale", "wpp", "g_q", "wq", "g_kv", "wk", "wv",
                               "convw", "convb", "wax", "bax", "lam", "wlp")]
    in_specs = ([tile(D_MODEL)] + [_layer_spec(c, layer) for c in consts] + [tab, tab]
                + [_layer_spec(c, layer) for c in consts2])
    out_shape = (jax.ShapeDtypeStruct((tok, MLA_HEADS * HEAD_PAD), BF16),
                 jax.ShapeDtypeStruct((tok, MLA_HEADS * HEAD_PAD), BF16),
                 jax.ShapeDtypeStruct((tok, MLA_HEADS * HEAD_PAD), BF16),
                 jax.ShapeDtypeStruct((tok, D_MODEL), F32),
                 jax.ShapeDtypeStruct((tok, D_MODEL), BF16))
    out_specs = (tile(MLA_HEADS * HEAD_PAD), tile(MLA_HEADS * HEAD_PAD), tile(MLA_HEADS * HEAD_PAD),
                 tile(D_MODEL), tile(D_MODEL))
    return pl.pallas_call(
        functools.partial(_mix_kernel, ts=ts),
        grid=(batch, ns),
        in_specs=in_specs,
        out_specs=out_specs,
        out_shape=out_shape,
        scratch_shapes=[pltpu.VMEM((HALO, POOL_WIDTH + LRU_WIDTH), F32),
                        pltpu.VMEM((8, LRU_WIDTH), F32)],
        compiler_params=_params(2),
        name="mix",
    )(x2, *consts, *tabs, *consts2)


def _attn_call(q, k, v, batch, seq):
    tq = min(ATTN_TILE, seq)
    nq = seq // tq
    hg = ATTN_HEADS_PER_STEP
    width = hg * HEAD_PAD
    o = jnp.zeros((batch, nq // 2, 2, tq, MLA_WIDTH), BF16)
    for pair in range(nq // 2):
        tiles = (pair, nq - 1 - pair)
        o = pl.pallas_call(
            functools.partial(_attn_kernel, tq=tq, heads=hg, tiles=tiles),
            grid=(batch, MLA_HEADS // hg),
            in_specs=[pl.BlockSpec((tq, width), lambda b, g, t=tiles[0]: (b * nq + t, g)),
                      pl.BlockSpec((tq, width), lambda b, g, t=tiles[1]: (b * nq + t, g)),
                      pl.BlockSpec((seq, width), lambda b, g: (b, g)),
                      pl.BlockSpec((seq, width), lambda b, g: (b, g)),
                      pl.BlockSpec(memory_space=pl.ANY)],
            out_specs=pl.BlockSpec((None, None, 2, tq, hg * MLA_V_DIM), lambda b, g, p=pair: (b, p, 0, 0, g)),
            out_shape=jax.ShapeDtypeStruct(o.shape, BF16),
            input_output_aliases={4: 0},
            scratch_shapes=[pltpu.VMEM((2, hg, tq, LANES), F32),
                            pltpu.VMEM((hg // 2, tq, 2 * LANES), F32),
                            pltpu.VMEM((hg // 2, tq, 2 * tq), BF16),
                            pltpu.VMEM((hg // 2, tq, 2 * LANES), F32),
                            pltpu.VMEM((hg // 2, tq, 2 * tq), BF16),
                            pltpu.VMEM((2, hg // 2, tq, 2 * LANES), F32)],
            compiler_params=_params(2),
            name=f"attn{pair}",
        )(q, q, k, v, o)
    return o


def _ffn_call(x2, o, part, g1, lw, layer, g_final, final):
    tok = x2.shape[0]
    tm = min(FFN_TILE, tok)
    tile = lambda w: pl.BlockSpec((tm, w), lambda t: (t, 0))
    consts = [lw[n] for n in ("wo", "wout", "g_ffn", "w1", "w2")]
    nq = o.shape[1] * 2
    assert o.shape[3] == tm

    def o_index(t):
        b, s = t // nq, t % nq
        return b, jnp.minimum(s, nq - 1 - s), s // (nq // 2), 0, 0

    o_spec = pl.BlockSpec((None, None, None, tm, MLA_WIDTH), o_index)
    return pl.pallas_call(
        functools.partial(_ffn_kernel, final=final),
        grid=(tok // tm,),
        in_specs=[tile(D_MODEL), o_spec, tile(D_MODEL), tile(D_MODEL)]
                 + [_layer_spec(c, layer) for c in consts] + [_layer_spec(g_final, 0)],
        out_specs=tile(D_MODEL),
        out_shape=jax.ShapeDtypeStruct((tok, D_MODEL), F32),
        compiler_params=_params(1),
        name="ffn",
    )(x2, o, part, g1, *consts, g_final)


def _block_diag(w):
    depth, g, c, _ = w.shape
    eye = jnp.eye(g, dtype=w.dtype)
    return jnp.einsum("lgcd,gh->lgchd", w, eye).reshape(depth, g * c, g * c)


def _rope_tables(seq):
    pos = jnp.arange(seq, dtype=F32)
    inv = ROPE_THETA ** (-jnp.arange(0, MLA_ROPE_DIM, 2, dtype=F32) / MLA_ROPE_DIM)
    ang = pos[:, None] * inv[None, :]
    cos, sin = jnp.cos(ang), jnp.sin(ang)
    cos_t = _head_layout(jnp.ones((seq, MLA_NOPE_DIM), F32), jnp.concatenate([cos, cos], axis=1))
    sin_t = _head_layout(jnp.zeros((seq, MLA_NOPE_DIM), F32), jnp.concatenate([-sin, sin], axis=1))
    return cos_t, sin_t


def _head_layout(nope, rope):
    half = MLA_ROPE_DIM // 2
    split = LANES // 2 - half
    tail = jnp.zeros(nope.shape[:-1] + (HEAD_PAD - MLA_QK_DIM,), nope.dtype)
    return jnp.concatenate([rope[..., :half], nope[..., :split], rope[..., half:], nope[..., split:], tail],
                           axis=-1)


def _prep_weights(w_in, w_pool_grp, pool_scale, w_pool_proj, g_mix, g_q, w_q_up, g_kv, w_kv_up, w_mla_o,
                  conv_w, conv_b, w_lru_a, b_lru_a, w_lru_x, b_lru_x, lru_lambda, w_lru_proj, b_gate,
                  w_out, g_ffn, w_ff1, w_ff2):
    depth = w_in.shape[0]
    row = lambda a: a[:, None, :]
    w_pool = w_in[:, :, :IN_POOL_END]
    w_qlat = w_in[:, :, IN_POOL_END:IN_Q_END]
    w_ckv = w_in[:, :, IN_Q_END:IN_Q_END + MLA_KV_RANK]
    w_kr = w_in[:, :, IN_Q_END + MLA_KV_RANK:IN_KV_END]
    w_lru = w_in[:, :, IN_KV_END:IN_LRU_END]
    w_gate = w_in[:, :, IN_LRU_END:]
    kr_blk = _head_layout(jnp.zeros(w_kr.shape[:-1] + (MLA_NOPE_DIM,), w_kr.dtype), w_kr)
    wsm = jnp.concatenate([w_pool, w_lru, w_qlat, w_ckv, kr_blk], axis=2).astype(BF16)
    wq4 = w_q_up.reshape(depth, MLA_Q_RANK, MLA_HEADS, MLA_QK_DIM)
    wq = _head_layout(wq4[..., :MLA_NOPE_DIM], wq4[..., MLA_NOPE_DIM:])
    wq = wq.reshape(depth, MLA_Q_RANK, MLA_HEADS * HEAD_PAD).astype(BF16)
    wkv = w_kv_up.reshape(depth, MLA_KV_RANK, MLA_HEADS, MLA_NOPE_DIM + MLA_V_DIM)
    wk = _head_layout(wkv[..., :MLA_NOPE_DIM], jnp.zeros(wkv.shape[:-1] + (MLA_ROPE_DIM,), wkv.dtype))
    wk = wk.reshape(depth, MLA_KV_RANK, MLA_HEADS * HEAD_PAD).astype(BF16)
    wv2 = wkv[..., MLA_NOPE_DIM:].reshape(depth, MLA_KV_RANK, MLA_HEADS // 2, 2, MLA_V_DIM)
    pad_v = HEAD_PAD - MLA_V_DIM
    wv_even = jnp.pad(wv2[:, :, :, 0], ((0, 0), (0, 0), (0, 0), (0, pad_v)))
    wv_odd = jnp.pad(wv2[:, :, :, 1], ((0, 0), (0, 0), (0, 0), (pad_v, 0)))
    wv = jnp.stack([wv_even, wv_odd], axis=3).reshape(depth, MLA_KV_RANK, MLA_HEADS * HEAD_PAD).astype(BF16)
    wax = (0.5 * jnp.concatenate([_block_diag(w_lru_a), _block_diag(w_lru_x)], axis=2)).astype(BF16)
    return dict(
        g_mix=row(g_mix), wsm=wsm, wg=(0.5 * w_gate).astype(BF16), bg=row(0.5 * b_gate),
        wpg=_block_diag(w_pool_grp).astype(BF16), pscale=row(pool_scale), wpp=w_pool_proj.astype(BF16),
        g_q=row(g_q), wq=wq, g_kv=row(g_kv), wk=wk, wv=wv,
        convw=conv_w, convb=row(conv_b), wax=wax,
        bax=row(0.5 * jnp.concatenate([b_lru_a, b_lru_x], axis=1)), lam=row(lru_lambda),
        wlp=w_lru_proj.astype(BF16),
        wo=w_mla_o.astype(BF16), wout=w_out.astype(BF16), g_ffn=row(g_ffn),
        w1=w_ff1.astype(BF16), w2=w_ff2.astype(BF16))


def kernel(x, g_mix, w_in, w_pool_grp, pool_scale, w_pool_proj, g_q, w_q_up, g_kv, w_kv_up, w_mla_o, conv_w, conv_b, w_lru_a, b_lru_a, w_lru_x, b_lru_x, lru_lambda, w_lru_proj, b_gate, w_out, g_ffn, w_ff1, w_ff2, g_final):
    batch, seq, d = x.shape
    assert d == D_MODEL and seq % min(SEQ_TILE, seq) == 0 and seq >= HALO
    assert (seq // min(ATTN_TILE, seq)) % 2 == 0 and FFN_TILE == ATTN_TILE
    depth = w_in.shape[0]
    weights = _prep_weights(w_in, w_pool_grp, pool_scale, w_pool_proj, g_mix, g_q, w_q_up, g_kv, w_kv_up,
                            w_mla_o, conv_w, conv_b, w_lru_a, b_lru_a, w_lru_x, b_lru_x, lru_lambda,
                            w_lru_proj, b_gate, w_out, g_ffn, w_ff1, w_ff2)
    tabs = _rope_tables(seq)
    gfin = g_final[None, None, :]
    x2 = x.reshape(batch * seq, d)
    for l in range(depth):
        q, k, v, part, g1 = _mix_call(x2, weights, l, tabs, batch, seq)
        o = _attn_call(q, k, v, batch, seq)
        x2 = _ffn_call(x2, o, part, g1, weights, l, gfin, final=(l == depth - 1))
    return x2.reshape(batch, seq, d)
```

```python
import functools

import jax
import jax.numpy as jnp
from jax import lax
from jax.experimental import pallas as pl
from jax.experimental.pallas import tpu as pltpu

F32 = jnp.float32
BF16 = jnp.bfloat16

D_MODEL = 1024
DEPTH = 4
POOL_GROUPS = 4
POOL_GROUP_DIM = 64
POOL_WIDTH = 256
POOL_WINDOWS = (2, 4, 8, 16)
MLA_HEADS = 8
MLA_NOPE_DIM = 64
MLA_ROPE_DIM = 32
MLA_V_DIM = 64
MLA_Q_RANK = 384
MLA_KV_RANK = 256
MLA_QK_DIM = MLA_NOPE_DIM + MLA_ROPE_DIM
MLA_WIDTH = MLA_HEADS * MLA_V_DIM
ROPE_THETA = 10000.0
LRU_WIDTH = 256
CONV_WIDTH = 4
LRU_C = 8.0
D_FF = 4 * D_MODEL
EPS = 1e-6
LOG2_E = 1.4426950408889634

IN_POOL_END = POOL_WIDTH
IN_Q_END = IN_POOL_END + MLA_Q_RANK
IN_KV_END = IN_Q_END + MLA_KV_RANK + MLA_ROPE_DIM
IN_LRU_END = IN_KV_END + LRU_WIDTH

LANES = 128
SUBLANES = 8
HEAD_PAD = LANES
HALO = 16
SMALL_COLS = POOL_WIDTH + LRU_WIDTH + MLA_Q_RANK + MLA_KV_RANK + LANES
VMEM_LIMIT_BYTES = 56 * 1024 * 1024
MASK_VALUE = -1e30

SEQ_TILE = 512
ATTN_TILE = 512
ATTN_HEADS_PER_STEP = 4
FFN_TILE = 512
FF_CHUNK = 1024


def _rms(x, g):
    return x * lax.rsqrt(jnp.mean(x * x, axis=-1, keepdims=True) + EPS) * g


def _sigmoid_of_twice(half_x):
    return 0.5 * jnp.tanh(half_x) + 0.5


def _rope(blk, cos, sin):
    return blk * cos + pltpu.roll(blk, LANES // 2, 1) * sin


def _mix_kernel(x_ref, gmix_ref, wsm_ref, wg_ref, bg_ref, cos_ref, sin_ref,
                wpg_ref, pscale_ref, wpp_ref, gq_ref, wq_ref, gkv_ref, wk_ref, wv_ref,
                convw_ref, convb_ref, wax_ref, bax_ref, lam_ref, wlp_ref,
                q_ref, k_ref, v_ref, part_ref, g1_ref,
                halo_ref, h_ref, *, ts):
    s = pl.program_id(1)

    @pl.when(s == 0)
    def _():
        halo_ref[...] = jnp.zeros_like(halo_ref)
        h_ref[...] = jnp.zeros_like(h_ref)

    hb = _rms(x_ref[...], gmix_ref[...]).astype(BF16)
    small = jnp.dot(hb, wsm_ref[...], preferred_element_type=F32)

    def gate_logit(j):
        return jnp.dot(hb, wg_ref[:, j * D_MODEL:(j + 1) * D_MODEL], preferred_element_type=F32)

    def gate(j, z):
        return _sigmoid_of_twice(z + bg_ref[:, j * D_MODEL:(j + 1) * D_MODEL])

    u2 = small[:, :POOL_WIDTH + LRU_WIDTH]
    ext = jnp.concatenate([halo_ref[...], u2], axis=0)
    halo_ref[...] = u2[ts - HALO:, :]
    ep = ext[:, :POOL_WIDTH]
    el = ext[:, POOL_WIDTH:]
    row = lax.broadcasted_iota(jnp.int32, (ts, 1), 0)

    cw = convw_ref[...]
    uc = (cw[3:4, :] * el + cw[2:3, :] * pltpu.roll(el, 1, 0)
          + cw[1:2, :] * pltpu.roll(el, 2, 0) + cw[0:1, :] * pltpu.roll(el, 3, 0))
    uc = uc[HALO:] + convb_ref[...]
    ri = _sigmoid_of_twice(jnp.dot(uc.astype(BF16), wax_ref[...], preferred_element_type=F32) + bax_ref[...])
    z_lru = gate_logit(2)
    g1_ref[...] = gate(1, gate_logit(1)).astype(BF16)
    z_pool = gate_logit(0)
    r = ri[:, :LRU_WIDTH]
    ig = ri[:, LRU_WIDTH:]
    nlam = -lam_ref[...]
    softplus = jnp.maximum(nlam, 0.0) + jnp.log1p(jnp.exp(-jnp.abs(nlam)))
    log_a = (-LRU_C) * r * softplus
    a_cum = jnp.exp(log_a)
    b_cum = jnp.sqrt(-jnp.tanh(log_a) * (a_cum * a_cum + 1.0)) * (ig * uc)
    sub = lax.broadcasted_iota(jnp.int32, (1, SUBLANES, 1), 1)
    a3 = a_cum.reshape(ts // SUBLANES, SUBLANES, LRU_WIDTH)
    b3 = b_cum.reshape(ts // SUBLANES, SUBLANES, LRU_WIDTH)
    k = 1
    while k < SUBLANES:
        keep = sub >= k
        a_sh = jnp.where(keep, pltpu.roll(a3, k, 1), 1.0)
        b_sh = jnp.where(keep, pltpu.roll(b3, k, 1), 0.0)
        b3 = a3 * b_sh + b3
        a3 = a3 * a_sh
        k *= 2
    carry = h_ref[...]
    tiles = []
    for t in range(ts // SUBLANES):
        h_t = a3[t] * carry + b3[t]
        tiles.append(h_t)
        carry = jnp.broadcast_to(h_t[SUBLANES - 1:, :], h_t.shape)
    hh = jnp.concatenate(tiles, axis=0)
    h_ref[...] = carry
    y_lru = jnp.dot(hh.astype(BF16), wlp_ref[...], preferred_element_type=F32)

    part_lru = gate(2, z_lru) * y_lru

    s2 = ep + pltpu.roll(ep, 1, 0)
    s4 = s2 + pltpu.roll(s2, 2, 0)
    s8 = s4 + pltpu.roll(s4, 4, 0)
    s16 = s8 + pltpu.roll(s8, 8, 0)
    lane = lax.broadcasted_iota(jnp.int32, (1, POOL_WIDTH), 1)
    grp = lane // POOL_GROUP_DIM
    wsum = jnp.where(grp == 0, s2, jnp.where(grp == 1, s4, jnp.where(grp == 2, s8, s16)))[HALO:]
    win = jnp.where(grp == 0, float(POOL_WINDOWS[0]),
                    jnp.where(grp == 1, float(POOL_WINDOWS[1]),
                              jnp.where(grp == 2, float(POOL_WINDOWS[2]), float(POOL_WINDOWS[3]))))
    tpos = (s * ts + row + 1).astype(F32)
    cnt = jnp.minimum(tpos, win)
    mixed = wsum / cnt - u2[:, :POOL_WIDTH]
    yp = jnp.dot(mixed.astype(BF16), wpg_ref[...], preferred_element_type=F32) * pscale_ref[...]
    y_pool = jnp.dot(yp.astype(BF16), wpp_ref[...], preferred_element_type=F32)
    part_ref[...] = gate(0, z_pool) * y_pool + part_lru

    cos = cos_ref[...]
    sin = sin_ref[...]
    q0 = POOL_WIDTH + LRU_WIDTH
    c0 = q0 + MLA_Q_RANK
    r0 = c0 + MLA_KV_RANK
    qn = _rms(small[:, q0:c0], gq_ref[...]) * (MLA_QK_DIM ** -0.5 * LOG2_E)
    qf = jnp.dot(qn.astype(BF16), wq_ref[...], preferred_element_type=F32)
    cn = _rms(small[:, c0:r0], gkv_ref[...]).astype(BF16)
    kf = jnp.dot(cn, wk_ref[...], preferred_element_type=F32)
    kr = _rope(small[:, r0:r0 + LANES], cos, sin)
    for hd in range(MLA_HEADS):
        sl = slice(hd * HEAD_PAD, (hd + 1) * HEAD_PAD)
        q_ref[:, sl] = _rope(qf[:, sl], cos, sin).astype(BF16)
        k_ref[:, sl] = (kf[:, sl] + kr).astype(BF16)
    vlane = lax.broadcasted_iota(jnp.int32, (1, MLA_HEADS * HEAD_PAD), 1)
    vpos = vlane % (2 * HEAD_PAD)
    vone = jnp.where((vpos == MLA_V_DIM) | (vpos == HEAD_PAD), 1.0, 0.0)
    v_ref[...] = (jnp.dot(cn, wv_ref[...], preferred_element_type=F32) + vone).astype(BF16)


def _attn_kernel(qa_ref, qb_ref, k_ref, v_ref, o_in_ref, o_ref, m_ref, alpha0_ref, p0_ref, alpha1_ref, p1_ref,
                 acc_ref, *, tq, heads, tiles):
    del o_in_ref
    q_refs = (qa_ref, qb_ref)
    bufs = ((p0_ref, alpha0_ref), (p1_ref, alpha1_ref))
    all_pairs = tuple(range(heads // 2))
    row = lax.broadcasted_iota(jnp.int32, (tq, tq), 0)
    col = lax.broadcasted_iota(jnp.int32, (tq, tq), 1)
    causal = col <= row
    zero_v = jnp.zeros((tq, HEAD_PAD), BF16)
    for t in range(2):
        for j in range(heads):
            m_ref[t, j] = jnp.full((tq, LANES), MASK_VALUE, F32)
        for a in all_pairs:
            acc_ref[t, a] = jnp.zeros((tq, 2 * LANES), F32)

    steps = []
    for t, tile in enumerate(tiles):
        steps.append((t, tile, True))
        steps.extend((t, c, False) for c in range(tile))

    def scores(n, pairs):
        t, c, masked = steps[n]
        p_ref, alpha_ref = bufs[n % 2]
        for j in [2 * a + half for a in pairs for half in (0, 1)]:
            hs = slice(j * HEAD_PAD, (j + 1) * HEAD_PAD)
            a, half = divmod(j, 2)
            kc = k_ref[c * tq:(c + 1) * tq, hs]
            sc = lax.dot_general(q_refs[t][:, hs], kc, (((1,), (1,)), ((), ())), preferred_element_type=F32)
            if masked:
                sc = jnp.where(causal, sc, MASK_VALUE)
            m_old = m_ref[t, j]
            m_new = jnp.maximum(m_old, jnp.max(sc, axis=-1, keepdims=True))
            p = jnp.exp2(sc - jnp.concatenate([m_new] * (tq // LANES), axis=1))
            p_ref[a, :, half * tq:(half + 1) * tq] = p.astype(BF16)
            alpha_ref[a, :, half * LANES:(half + 1) * LANES] = jnp.exp2(m_old - m_new)
            m_ref[t, j] = m_new

    def accumulate(n, pairs):
        t, c, _ = steps[n]
        p_ref, alpha_ref = bufs[n % 2]
        for a in pairs:
            v_even = v_ref[c * tq:(c + 1) * tq, (2 * a) * HEAD_PAD:(2 * a + 1) * HEAD_PAD]
            v_odd = v_ref[c * tq:(c + 1) * tq, (2 * a + 1) * HEAD_PAD:(2 * a + 2) * HEAD_PAD]
            v2 = jnp.concatenate([jnp.concatenate([v_even, zero_v], axis=1),
                                  jnp.concatenate([zero_v, v_odd], axis=1)], axis=0)
            acc_ref[t, a] = alpha_ref[a] * acc_ref[t, a] + jnp.dot(p_ref[a], v2, preferred_element_type=F32)

    lane = lax.broadcasted_iota(jnp.int32, (1, LANES), 1)

    def finish(t):
        for a in all_pairs:
            even = acc_ref[t, a][:, :LANES]
            odd = acc_ref[t, a][:, LANES:]
            o_even = even / even[:, MLA_V_DIM:MLA_V_DIM + 1]
            o_odd = odd / odd[:, 0:1]
            o_ref[t, :, a * LANES:(a + 1) * LANES] = jnp.where(lane < MLA_V_DIM, o_even, o_odd).astype(BF16)

    scores(0, all_pairs)
    for n in range(1, len(steps)):
        for a in all_pairs:
            scores(n, (a,))
            accumulate(n - 1, (a,))
        if steps[n][0] != steps[n - 1][0]:
            finish(steps[n - 1][0])
    accumulate(len(steps) - 1, all_pairs)
    finish(steps[-1][0])


def _ffn_kernel(x_ref, o_ref, part_ref, g1_ref, wo_ref, wout_ref, gffn_ref, w1_ref, w2_ref, gfin_ref,
                out_ref, *, final):
    y_mla = jnp.dot(o_ref[...], wo_ref[...], preferred_element_type=F32)
    merged = part_ref[...] + g1_ref[...].astype(F32) * y_mla
    x1 = x_ref[...] + jnp.dot(merged.astype(BF16), wout_ref[...], preferred_element_type=F32)
    hb = _rms(x1, gffn_ref[...]).astype(BF16)
    acc = x1
    for c in range(D_FF // FF_CHUNK):
        cs = slice(c * FF_CHUNK, (c + 1) * FF_CHUNK)
        f = jnp.dot(hb, w1_ref[:, cs], preferred_element_type=F32)
        f = jnp.square(jnp.maximum(f, 0.0)).astype(BF16)
        acc = acc + jnp.dot(f, w2_ref[cs, :], preferred_element_type=F32)
    if final:
        acc = _rms(acc, gfin_ref[...])
    out_ref[...] = acc


def _layer_spec(stacked, layer):
    _, rows, cols = stacked.shape
    return pl.BlockSpec((None, rows, cols), lambda *_: (layer, 0, 0), pipeline_mode=pl.Buffered(1))


def _params(n_axes):
    return pltpu.CompilerParams(dimension_semantics=("arbitrary",) * n_axes,
                                vmem_limit_bytes=VMEM_LIMIT_BYTES)


def _mix_call(x2, lw, layer, tabs, batch, seq):
    ts = min(SEQ_TILE, seq)
    ns = seq // ts
    tok = batch * seq
    tile = lambda w: pl.BlockSpec((ts, w), lambda b, s: (b * ns + s, 0))
    tab = pl.BlockSpec((ts, LANES), lambda b, s: (s, 0))
    consts = [lw[n] for n in ("g_mix", "wsm", "wg", "bg")]
    consts2 = [lw[n] for n in ("wpg", "pscale", "wpp", "g_q", "wq", "g_kv", "wk", "wv",
                               "convw", "convb", "wax", "bax", "lam", "wlp")]
    in_specs = ([tile(D_MODEL)] + [_layer_spec(c, layer) for c in consts] + [tab, tab]
                + [_layer_spec(c, layer) for c in consts2])
    out_shape = (jax.ShapeDtypeStruct((tok, MLA_HEADS * HEAD_PAD), BF16),
                 jax.ShapeDtypeStruct((tok, MLA_HEADS * HEAD_PAD), BF16),
                 jax.ShapeDtypeStruct((tok, MLA_HEADS * HEAD_PAD), BF16),
                 jax.ShapeDtypeStruct((tok, D_MODEL), F32),
                 jax.ShapeDtypeStruct((tok, D_MODEL), BF16))
    out_specs = (tile(MLA_HEADS * HEAD_PAD), tile(MLA_HEADS * HEAD_PAD), tile(MLA_HEADS * HEAD_PAD),
                 tile(D_MODEL), tile(D_MODEL))
    return pl.pallas_call(
        functools.partial(_mix_kernel, ts=ts),
        grid=(batch, ns),
        in_specs=in_specs,
        out_specs=out_specs,
        out_shape=out_shape,
        scratch_shapes=[pltpu.VMEM((HALO, POOL_WIDTH + LRU_WIDTH), F32),
                        pltpu.VMEM((8, LRU_WIDTH), F32)],
        compiler_params=_params(2),
        name="mix",
    )(x2, *consts, *tabs, *consts2)


def _attn_call(q, k, v, batch, seq):
    tq = min(ATTN_TILE, seq)
    nq = seq // tq
    hg = ATTN_HEADS_PER_STEP
    width = hg * HEAD_PAD
    o = jnp.zeros((batch, nq // 2, 2, tq, MLA_WIDTH), BF16)
    for pair in range(nq // 2):
        tiles = (pair, nq - 1 - pair)
        o = pl.pallas_call(
            functools.partial(_attn_kernel, tq=tq, heads=hg, tiles=tiles),
            grid=(batch, MLA_HEADS // hg),
            in_specs=[pl.BlockSpec((tq, width), lambda b, g, t=tiles[0]: (b * nq + t, g)),
                      pl.BlockSpec((tq, width), lambda b, g, t=tiles[1]: (b * nq + t, g)),
                      pl.BlockSpec((seq, width), lambda b, g: (b, g)),
                      pl.BlockSpec((seq, width), lambda b, g: (b, g)),
                      pl.BlockSpec(memory_space=pl.ANY)],
            out_specs=pl.BlockSpec((None, None, 2, tq, hg * MLA_V_DIM), lambda b, g, p=pair: (b, p, 0, 0, g)),
            out_shape=jax.ShapeDtypeStruct(o.shape, BF16),
            input_output_aliases={4: 0},
            scratch_shapes=[pltpu.VMEM((2, hg, tq, LANES), F32),
                            pltpu.VMEM((hg // 2, tq, 2 * LANES), F32),
                            pltpu.VMEM((hg // 2, tq, 2 * tq), BF16),
                            pltpu.VMEM((hg // 2, tq, 2 * LANES), F32),
                            pltpu.VMEM((hg // 2, tq, 2 * tq), BF16),
                            pltpu.VMEM((2, hg // 2, tq, 2 * LANES), F32)],
            compiler_params=_params(2),
            name=f"attn{pair}",
        )(q, q, k, v, o)
    return o


def _ffn_call(x2, o, part, g1, lw, layer, g_final, final):
    tok = x2.shape[0]
    tm = min(FFN_TILE, tok)
    tile = lambda w: pl.BlockSpec((tm, w), lambda t: (t, 0))
    consts = [lw[n] for n in ("wo", "wout", "g_ffn", "w1", "w2")]
    nq = o.shape[1] * 2
    assert o.shape[3] == tm

    def o_index(t):
        b, s = t // nq, t % nq
        return b, jnp.minimum(s, nq - 1 - s), s // (nq // 2), 0, 0

    o_spec = pl.BlockSpec((None, None, None, tm, MLA_WIDTH), o_index)
    return pl.pallas_call(
        functools.partial(_ffn_kernel, final=final),
        grid=(tok // tm,),
        in_specs=[tile(D_MODEL), o_spec, tile(D_MODEL), tile(D_MODEL)]
                 + [_layer_spec(c, layer) for c in consts] + [_layer_spec(g_final, 0)],
        out_specs=tile(D_MODEL),
        out_shape=jax.ShapeDtypeStruct((tok, D_MODEL), F32),
        compiler_params=_params(1),
        name="ffn",
    )(x2, o, part, g1, *consts, g_final)


def _block_diag(w):
    depth, g, c, _ = w.shape
    eye = jnp.eye(g, dtype=w.dtype)
    return jnp.einsum("lgcd,gh->lgchd", w, eye).reshape(depth, g * c, g * c)


def _rope_tables(seq):
    pos = jnp.arange(seq, dtype=F32)
    inv = ROPE_THETA ** (-jnp.arange(0, MLA_ROPE_DIM, 2, dtype=F32) / MLA_ROPE_DIM)
    ang = pos[:, None] * inv[None, :]
    cos, sin = jnp.cos(ang), jnp.sin(ang)
    cos_t = _head_layout(jnp.ones((seq, MLA_NOPE_DIM), F32), jnp.concatenate([cos, cos], axis=1))
    sin_t = _head_layout(jnp.zeros((seq, MLA_NOPE_DIM), F32), jnp.concatenate([-sin, sin], axis=1))
    return cos_t, sin_t


def _head_layout(nope, rope):
    half = MLA_ROPE_DIM // 2
    split = LANES // 2 - half
    tail = jnp.zeros(nope.shape[:-1] + (HEAD_PAD - MLA_QK_DIM,), nope.dtype)
    return jnp.concatenate([rope[..., :half], nope[..., :split], rope[..., half:], nope[..., split:], tail],
                           axis=-1)


def _prep_weights(w_in, w_pool_grp, pool_scale, w_pool_proj, g_mix, g_q, w_q_up, g_kv, w_kv_up, w_mla_o,
                  conv_w, conv_b, w_lru_a, b_lru_a, w_lru_x, b_lru_x, lru_lambda, w_lru_proj, b_gate,
                  w_out, g_ffn, w_ff1, w_ff2):
    depth = w_in.shape[0]
    row = lambda a: a[:, None, :]
    w_pool = w_in[:, :, :IN_POOL_END]
    w_qlat = w_in[:, :, IN_POOL_END:IN_Q_END]
    w_ckv = w_in[:, :, IN_Q_END:IN_Q_END + MLA_KV_RANK]
    w_kr = w_in[:, :, IN_Q_END + MLA_KV_RANK:IN_KV_END]
    w_lru = w_in[:, :, IN_KV_END:IN_LRU_END]
    w_gate = w_in[:, :, IN_LRU_END:]
    kr_blk = _head_layout(jnp.zeros(w_kr.shape[:-1] + (MLA_NOPE_DIM,), w_kr.dtype), w_kr)
    wsm = jnp.concatenate([w_pool, w_lru, w_qlat, w_ckv, kr_blk], axis=2).astype(BF16)
    wq4 = w_q_up.reshape(depth, MLA_Q_RANK, MLA_HEADS, MLA_QK_DIM)
    wq = _head_layout(wq4[..., :MLA_NOPE_DIM], wq4[..., MLA_NOPE_DIM:])
    wq = wq.reshape(depth, MLA_Q_RANK, MLA_HEADS * HEAD_PAD).astype(BF16)
    wkv = w_kv_up.reshape(depth, MLA_KV_RANK, MLA_HEADS, MLA_NOPE_DIM + MLA_V_DIM)
    wk = _head_layout(wkv[..., :MLA_NOPE_DIM], jnp.zeros(wkv.shape[:-1] + (MLA_ROPE_DIM,), wkv.dtype))
    wk = wk.reshape(depth, MLA_KV_RANK, MLA_HEADS * HEAD_PAD).astype(BF16)
    wv2 = wkv[..., MLA_NOPE_DIM:].reshape(depth, MLA_KV_RANK, MLA_HEADS // 2, 2, MLA_V_DIM)
    pad_v = HEAD_PAD - MLA_V_DIM
    wv_even = jnp.pad(wv2[:, :, :, 0], ((0, 0), (0, 0), (0, 0), (0, pad_v)))
    wv_odd = jnp.pad(wv2[:, :, :, 1], ((0, 0), (0, 0), (0, 0), (pad_v, 0)))
    wv = jnp.stack([wv_even, wv_odd], axis=3).reshape(depth, MLA_KV_RANK, MLA_HEADS * HEAD_PAD).astype(BF16)
    wax = (0.5 * jnp.concatenate([_block_diag(w_lru_a), _block_diag(w_lru_x)], axis=2)).astype(BF16)
    return dict(
        g_mix=row(g_mix), wsm=wsm, wg=(0.5 * w_gate).astype(BF16), bg=row(0.5 * b_gate),
        wpg=_block_diag(w_pool_grp).astype(BF16), pscale=row(pool_scale), wpp=w_pool_proj.astype(BF16),
        g_q=row(g_q), wq=wq, g_kv=row(g_kv), wk=wk, wv=wv,
        convw=conv_w, convb=row(conv_b), wax=wax,
        bax=row(0.5 * jnp.concatenate([b_lru_a, b_lru_x], axis=1)), lam=row(lru_lambda),
        wlp=w_lru_proj.astype(BF16),
        wo=w_mla_o.astype(BF16), wout=w_out.astype(BF16), g_ffn=row(g_ffn),
        w1=w_ff1.astype(BF16), w2=w_ff2.astype(BF16))


def kernel(x, g_mix, w_in, w_pool_grp, pool_scale, w_pool_proj, g_q, w_q_up, g_kv, w_kv_up, w_mla_o, conv_w, conv_b, w_lru_a, b_lru_a, w_lru_x, b_lru_x, lru_lambda, w_lru_proj, b_gate, w_out, g_ffn, w_ff1, w_ff2, g_final):
    batch, seq, d = x.shape
    assert d == D_MODEL and seq % min(SEQ_TILE, seq) == 0 and seq >= HALO
    assert (seq // min(ATTN_TILE, seq)) % 2 == 0 and FFN_TILE == ATTN_TILE
    depth = w_in.shape[0]
    weights = _prep_weights(w_in, w_pool_grp, pool_scale, w_pool_proj, g_mix, g_q, w_q_up, g_kv, w_kv_up,
                            w_mla_o, conv_w, conv_b, w_lru_a, b_lru_a, w_lru_x, b_lru_x, lru_lambda,
                            w_lru_proj, b_gate, w_out, g_ffn, w_ff1, w_ff2)
    tabs = _rope_tables(seq)
    gfin = g_final[None, None, :]
    x2 = x.reshape(batch * seq, d)
    for l in range(depth):
        q, k, v, part, g1 = _mix_call(x2, weights, l, tabs, batch, seq)
        o = _attn_call(q, k, v, batch, seq)
        x2 = _ffn_call(x2, o, part, g1, weights, l, gfin, final=(l == depth - 1))
    return x2.reshape(batch, seq, d)
```
